```python
import jax, jax.numpy as jnp
from jax import lax
import numpy as np

D_MODEL = 4096
BATCH = 8
SEQ = 2048
DEPTH = 4

N_A = DEPTH // 2
N_B = DEPTH - N_A
CONV_W = 3
N_HEADS = 64
N_KV_HEADS = 8
HEAD_DIM = D_MODEL // N_HEADS
GROUP = N_HEADS // N_KV_HEADS
WINDOW = 128
BLOCK = WINDOW
N_EXPERTS = 32
TOP_K = 4
D_FF = 384
SWIGLU_ALPHA = 1.702
SWIGLU_LIMIT = 7.0
ADA_RANK = 256
NORM_EPS = 1e-5
QK_EPS = 1e-6

kernel_name = 'yoco_shortconv_swa_sink_moe_adaln'


def rms_norm(x, gain, eps):
    xf = x.astype(jnp.float32)
    y = xf * lax.rsqrt(jnp.mean(xf * xf, axis=-1, keepdims=True) + eps)
    return (y * gain.astype(jnp.float32)).astype(x.dtype)


def modulate(n, shift, scale):
    return n * (1 + scale[:, None, :]) + shift[:, None, :]


def adaln(c, w_down, w_up, b, n_chunks):
    m = (jax.nn.silu(c) @ w_down) @ w_up + b
    return jnp.split(m, n_chunks, axis=-1)


def alibi_slopes():
    h = np.arange(1, N_HEADS + 1, dtype=np.float32)
    return jnp.asarray(np.power(2.0, -8.0 * h / N_HEADS).astype(np.float32))


def short_conv_mixer(n, w_in, conv_w, w_out):
    seq = n.shape[1]
    b_gate, c_gate, v = jnp.split(n @ w_in, 3, axis=-1)
    u = c_gate * v
    u_pad = jnp.pad(u, ((0, 0), (CONV_W - 1, 0), (0, 0)))
    conv = sum(conv_w[j] * u_pad[:, j:j + seq] for j in range(CONV_W))
    return (b_gate * conv) @ w_out


def band_blocks(t):
    b, seq = t.shape[0], t.shape[1]
    nb = seq // BLOCK
    blocks = t.reshape(b, nb, BLOCK, N_KV_HEADS, HEAD_DIM)
    prev = jnp.concatenate([jnp.zeros_like(blocks[:, :1]), blocks[:, :-1]], axis=1)
    return jnp.concatenate([prev, blocks], axis=2)


def shared_kv(h, c, kv_norm, kv_mod_down, kv_mod_up, kv_mod_b, w_kv, k_norm):
    b, seq, _ = h.shape
    shift, scale = adaln(c, kv_mod_down, kv_mod_up, kv_mod_b, 2)
    n = modulate(rms_norm(h, kv_norm, NORM_EPS), shift, scale)
    k, v = jnp.split(n @ w_kv, 2, axis=-1)
    k = rms_norm(k.reshape(b, seq, N_KV_HEADS, HEAD_DIM), k_norm, QK_EPS)
    v = v.reshape(b, seq, N_KV_HEADS, HEAD_DIM)
    return band_blocks(k), band_blocks(v)


def swa_sink_attention(n, w_q, q_norm, sinks, w_o, k_band, v_band):
    b, seq, _ = n.shape
    nb = seq // BLOCK
    q = rms_norm((n @ w_q).reshape(b, seq, N_KV_HEADS, GROUP, HEAD_DIM), q_norm, QK_EPS)
    q_blocks = q.reshape(b, nb, BLOCK, N_KV_HEADS, GROUP, HEAD_DIM)
    slopes = alibi_slopes().reshape(N_KV_HEADS, GROUP)
    qi = jnp.arange(BLOCK)[:, None]
    kj = jnp.arange(2 * BLOCK)[None, :]
    dist = BLOCK + qi - kj
    in_window = (dist >= 0) & (dist < WINDOW)
    alibi = -slopes[:, :, None, None] * dist.astype(jnp.float32)[None, None]
    sink = sinks.reshape(N_KV_HEADS, GROUP).astype(jnp.float32)
    scale = HEAD_DIM ** -0.5

    def one_block(args):
        qb, kb, vb, blk = args
        s = jnp.einsum('bqkgd,bskd->bkgqs', qb, kb).astype(jnp.float32) * scale + alibi
        valid = in_window & (blk * BLOCK - BLOCK + kj >= 0)
        s = jnp.where(valid, s, -jnp.inf)
        sink_col = jnp.broadcast_to(sink[None, :, :, None, None], s.shape[:-1] + (1,))
        p = jax.nn.softmax(jnp.concatenate([s, sink_col], axis=-1), axis=-1)[..., :-1]
        return jnp.einsum('bkgqs,bskd->bqkgd', p.astype(vb.dtype), vb)

    out = lax.map(one_block, (jnp.swapaxes(q_blocks, 0, 1), jnp.swapaxes(k_band, 0, 1),
                              jnp.swapaxes(v_band, 0, 1), jnp.arange(nb)))
    out = jnp.swapaxes(out, 0, 1).reshape(b, seq, N_HEADS * HEAD_DIM)
    return out @ w_o


def moe(n, router_w, router_b, w_gu, b_gu, w_d, b_d):
    b, seq, d = n.shape
    t = n.reshape(b * seq, d)
    logits = (t @ router_w + router_b).astype(jnp.float32)
    top_val, top_idx = lax.top_k(logits, TOP_K)
    top_w = jax.nn.softmax(top_val, axis=-1)
    combine = jnp.einsum('tk,tke->te', top_w,
                         jax.nn.one_hot(top_idx, N_EXPERTS, dtype=jnp.float32)).astype(t.dtype)

    def expert_step(acc, args):
        wgu, bgu, wd, bd, wt = args
        x_glu, x_lin = jnp.split(t @ wgu + bgu, 2, axis=-1)
        x_glu = jnp.minimum(x_glu, SWIGLU_LIMIT)
        x_lin = jnp.clip(x_lin, -SWIGLU_LIMIT, SWIGLU_LIMIT)
        hid = x_glu * jax.nn.sigmoid(SWIGLU_ALPHA * x_glu) * (x_lin + 1)
        return acc + wt[:, None] * (hid @ wd + bd), None

    y, _ = lax.scan(expert_step, jnp.zeros_like(t), (w_gu, b_gu, w_d, b_d, combine.T))
    return y.reshape(b, seq, d)


def setup_inputs(seed: int = 0) -> dict:
    key = jax.random.key(seed)
    ks = jax.random.split(key, 26)
    D, R, E, F = D_MODEL, ADA_RANK, N_EXPERTS, D_FF
    nrm = jax.random.normal
    f32 = jnp.float32
    return {
        'x': nrm(ks[0], (BATCH, SEQ, D), f32),
        'c': nrm(ks[1], (BATCH, D), f32),
        'norm_mix': 1.0 + 0.02 * nrm(ks[2], (DEPTH, D), f32),
        'norm_ffn': 1.0 + 0.02 * nrm(ks[3], (DEPTH, D), f32),
        'mod_down': nrm(ks[4], (DEPTH, D, R), f32) * D ** -0.5,
        'mod_up': nrm(ks[5], (DEPTH, R, 6 * D), f32) * (0.3 * R ** -0.5),
        'mod_b': 0.1 * nrm(ks[6], (DEPTH, 6 * D), f32),
        'a_w_in': nrm(ks[7], (N_A, D, 3 * D), f32) * D ** -0.5,
        'a_conv': nrm(ks[8], (N_A, CONV_W, D), f32) * CONV_W ** -0.5,
        'a_w_out': nrm(ks[9], (N_A, D, D), f32) * D ** -0.5,
        'kv_norm': 1.0 + 0.02 * nrm(ks[10], (D,), f32),
        'kv_mod_down': nrm(ks[11], (D, R), f32) * D ** -0.5,
        'kv_mod_up': nrm(ks[12], (R, 2 * D), f32) * (0.3 * R ** -0.5),
        'kv_mod_b': 0.1 * nrm(ks[13], (2 * D,), f32),
        'w_kv': nrm(ks[14], (D, 2 * N_KV_HEADS * HEAD_DIM), f32) * D ** -0.5,
        'k_norm': 1.0 + 0.02 * nrm(ks[15], (HEAD_DIM,), f32),
        'b_w_q': nrm(ks[16], (N_B, D, N_HEADS * HEAD_DIM), f32) * D ** -0.5,
        'b_q_norm': 1.0 + 0.02 * nrm(ks[17], (N_B, HEAD_DIM), f32),
        'b_sinks': nrm(ks[18], (N_B, N_HEADS), f32),
        'b_w_o': nrm(ks[19], (N_B, N_HEADS * HEAD_DIM, D), f32) * (N_HEADS * HEAD_DIM) ** -0.5,
        'router_w': nrm(ks[20], (DEPTH, D, E), f32) * D ** -0.5,
        'router_b': 0.01 * nrm(ks[21], (DEPTH, E), f32),
        'moe_w_gu': nrm(ks[22], (DEPTH, E, D, 2 * F), f32) * D ** -0.5,
        'moe_b_gu': 0.01 * nrm(ks[23], (DEPTH, E, 2 * F), f32),
        'moe_w_down': nrm(ks[24], (DEPTH, E, F, D), f32) * F ** -0.5,
        'moe_b_down': 0.01 * nrm(ks[25], (DEPTH, E, D), f32),
    }


def reference(x, c, norm_mix, norm_ffn, mod_down, mod_up, mod_b, a_w_in, a_conv, a_w_out,
              kv_norm, kv_mod_down, kv_mod_up, kv_mod_b, w_kv, k_norm,
              b_w_q, b_q_norm, b_sinks, b_w_o,
              router_w, router_b, moe_w_gu, moe_b_gu, moe_w_down, moe_b_down):
    h = x
    k_band = None
    v_band = None
    for layer in range(DEPTH):
        sh_m, sc_m, g_m, sh_f, sc_f, g_f = adaln(c, mod_down[layer], mod_up[layer], mod_b[layer], 6)
        n = modulate(rms_norm(h, norm_mix[layer], NORM_EPS), sh_m, sc_m)
        if layer < N_A:
            mix = short_conv_mixer(n, a_w_in[layer], a_conv[layer], a_w_out[layer])
        else:
            if layer == N_A:
                k_band, v_band = shared_kv(h, c, kv_norm, kv_mod_down, kv_mod_up, kv_mod_b, w_kv, k_norm)
            i = layer - N_A
            mix = swa_sink_attention(n, b_w_q[i], b_q_norm[i], b_sinks[i], b_w_o[i], k_band, v_band)
        h = h + g_m[:, None, :] * mix
        n = modulate(rms_norm(h, norm_ffn[layer], NORM_EPS), sh_f, sc_f)
        h = h + g_f[:, None, :] * moe(n, router_w[layer], router_b[layer], moe_w_gu[layer],
                                      moe_b_gu[layer], moe_w_down[layer], moe_b_down[layer])
    return h
```

```python
import functools

import numpy as np
import jax
import jax.numpy as jnp
from jax import lax
from jax.experimental import pallas as pl
from jax.experimental.pallas import tpu as pltpu

WINDOW = 128
TOP_K = 4
CONV_W = 3
NORM_EPS = 1e-5
QK_EPS = 1e-6
SWIGLU_ALPHA = 1.702
SWIGLU_LIMIT = 7.0
LANES = 128
SUBLANES = 8
NEG_BIG = -1e30
MIB = 1024 * 1024

F32 = jnp.float32
BF16 = jnp.bfloat16
U32 = jnp.uint32
I32 = jnp.int32
HI_MASK = np.uint32(0xFFFF0000)


def _tile(total, pref):
    t = min(total, pref)
    assert total % t == 0, (total, pref)
    return t


def _params(sem, vmem_mib):
    return pltpu.CompilerParams(dimension_semantics=sem, vmem_limit_bytes=vmem_mib * MIB)


def _pack_bf16_pair(y):
    n = y.shape[1] // 2
    lo = lax.bitcast_convert_type(y[:, :n].astype(BF16).astype(F32), U32) >> 16
    hi = lax.bitcast_convert_type(y[:, n:].astype(BF16).astype(F32), U32) & HI_MASK
    return lo | hi


def _unpack_bf16_pair(w):
    lo = lax.bitcast_convert_type(w << 16, F32)
    hi = lax.bitcast_convert_type(w & HI_MASK, F32)
    return lo, hi


def _adaln_kernel(c_ref, wd_ref, wu_ref, b_ref, o_ref, d_ref):
    @pl.when(pl.program_id(1) == 0)
    def _():
        c = c_ref[...]
        s = c * jax.nn.sigmoid(c)
        d_ref[...] = jnp.dot(s, wd_ref[0], preferred_element_type=F32, precision=lax.Precision.HIGHEST)

    o_ref[0] = jnp.dot(d_ref[...], wu_ref[0], preferred_element_type=F32,
                       precision=lax.Precision.HIGHEST) + b_ref[0]


def _adaln(c, w_down, w_up, b):
    nl, d, r = w_down.shape
    n = w_up.shape[2]
    bsz = c.shape[0]
    tn = _tile(n, min(d, 2048))
    return pl.pallas_call(
        _adaln_kernel,
        grid=(nl, n // tn),
        in_specs=[
            pl.BlockSpec((bsz, d), lambda l, j: (0, 0)),
            pl.BlockSpec((1, d, r), lambda l, j: (l, 0, 0)),
            pl.BlockSpec((1, r, tn), lambda l, j: (l, 0, j)),
            pl.BlockSpec((1, 1, tn), lambda l, j: (l, 0, j)),
        ],
        out_specs=pl.BlockSpec((1, bsz, tn), lambda l, j: (l, 0, j)),
        out_shape=jax.ShapeDtypeStruct((nl, bsz, n), F32),
        scratch_shapes=[pltpu.VMEM((bsz, r), F32)],
        compiler_params=_params(("arbitrary", "arbitrary"), 32),
        name="adaln",
    )(c, w_down, w_up, b.reshape(nl, 1, n))


def _norm_mod_value(h_ref, g_ref, sc_ref, sh_ref):
    x = h_ref[...]
    ms = jnp.mean(x * x, axis=-1, keepdims=True)
    y = x * lax.rsqrt(ms + NORM_EPS) * g_ref[...]
    return y * (1.0 + sc_ref[0]) + sh_ref[0]


def _norm_kernel(h_ref, g_ref, sc_ref, sh_ref, o_ref):
    o_ref[...] = _norm_mod_value(h_ref, g_ref, sc_ref, sh_ref).astype(BF16)


def _norm_mod(h, gain, scale, shift, seq):
    m, d = h.shape
    bsz = scale.shape[0]
    bm = _tile(seq, 256)
    row_b = lambda i: ((i * bm) // seq, 0, 0)
    return pl.pallas_call(
        _norm_kernel,
        grid=(m // bm,),
        in_specs=[
            pl.BlockSpec((bm, d), lambda i: (i, 0)),
            pl.BlockSpec((1, d), lambda i: (0, 0)),
            pl.BlockSpec((1, 1, d), row_b),
            pl.BlockSpec((1, 1, d), row_b),
        ],
        out_specs=pl.BlockSpec((bm, d), lambda i: (i, 0)),
        out_shape=jax.ShapeDtypeStruct((m, d), BF16),
        compiler_params=_params(("arbitrary",), 40),
        name="norm_mod",
    )(h, gain.reshape(1, d), scale.reshape(bsz, 1, d), shift.reshape(bsz, 1, d))


def _norm_router_kernel(h_ref, g_ref, sc_ref, sh_ref, rw_ref, rb_ref,
                        np_ref, ri_ref, wt_ref, cnt_ref, carry_ref):
    @pl.when(pl.program_id(0) == 0)
    def _():
        carry_ref[...] = jnp.zeros_like(carry_ref)

    y = _norm_mod_value(h_ref, g_ref, sc_ref, sh_ref)
    np_ref[...] = _pack_bf16_pair(y)
    logits = jnp.dot(y.astype(BF16), rw_ref[...], preferred_element_type=F32) + rb_ref[...]
    bm = logits.shape[0]
    lane = lax.broadcasted_iota(I32, logits.shape, 1)

    l = logits
    sels, vals, idxs = [], [], []
    for _ in range(TOP_K):
        mx = jnp.max(l, axis=-1, keepdims=True)
        idx = jnp.min(jnp.where(l == mx, lane, LANES), axis=-1, keepdims=True)
        sel = lane == idx
        l = jnp.where(sel, -jnp.inf, l)
        sels.append(sel)
        vals.append(mx)
        idxs.append(idx)
    exps = [jnp.exp(v - vals[0]) for v in vals]
    denom = exps[0]
    for e in exps[1:]:
        denom = denom + e

    mask = sels[0].astype(F32)
    for sel in sels[1:]:
        mask = mask + sel.astype(F32)
    rr = lax.broadcasted_iota(I32, (bm, bm), 0)
    cc = lax.broadcasted_iota(I32, (bm, bm), 1)
    tri = (cc < rr).astype(BF16)
    carry = carry_ref[0:1, :]
    prefix = jnp.dot(tri, mask.astype(BF16), preferred_element_type=F32) + carry

    ri = jnp.zeros(logits.shape, I32)
    wt = jnp.zeros(logits.shape, F32)
    for k in range(TOP_K):
        rank = jnp.sum(jnp.where(sels[k], prefix, 0.0), axis=-1, keepdims=True).astype(I32)
        ri = jnp.where(lane == k, idxs[k], ri)
        ri = jnp.where(lane == TOP_K + k, rank, ri)
        wt = jnp.where(lane == k, exps[k] / denom, wt)
    ri_ref[...] = ri
    wt_ref[...] = wt
    new_carry = jnp.broadcast_to(carry + jnp.sum(mask, axis=0, keepdims=True), carry_ref.shape)
    carry_ref[...] = new_carry
    cnt_ref[...] = new_carry


def _norm_router(h, gain, scale, shift, rw, rb, seq):
    m, d = h.shape
    bsz = scale.shape[0]
    bm = _tile(seq, 256)
    row_b = lambda i: ((i * bm) // seq, 0, 0)
    return pl.pallas_call(
        _norm_router_kernel,
        grid=(m // bm,),
        in_specs=[
            pl.BlockSpec((bm, d), lambda i: (i, 0)),
            pl.BlockSpec((1, d), lambda i: (0, 0)),
            pl.BlockSpec((1, 1, d), row_b),
            pl.BlockSpec((1, 1, d), row_b),
            pl.BlockSpec((d, LANES), lambda i: (0, 0)),
            pl.BlockSpec((1, LANES), lambda i: (0, 0)),
        ],
        out_specs=[
            pl.BlockSpec((bm, d // 2), lambda i: (i, 0)),
            pl.BlockSpec((bm, LANES), lambda i: (i, 0)),
            pl.BlockSpec((bm, LANES), lambda i: (i, 0)),
            pl.BlockSpec((SUBLANES, LANES), lambda i: (0, 0)),
        ],
        out_shape=[
            jax.ShapeDtypeStruct((m, d // 2), U32),
            jax.ShapeDtypeStruct((m, LANES), I32),
            jax.ShapeDtypeStruct((m, LANES), F32),
            jax.ShapeDtypeStruct((SUBLANES, LANES), F32),
        ],
        scratch_shapes=[pltpu.VMEM((SUBLANES, LANES), F32)],
        compiler_params=_params(("arbitrary",), 40),
        name="norm_router",
    )(h, gain.reshape(1, d), scale.reshape(bsz, 1, d), shift.reshape(bsz, 1, d), rw, rb)


def _conv_in_kernel(n_ref, wb_ref, wc_ref, wv_ref, cw_ref, y_ref, tail_ref, *, seq):
    i = pl.program_id(0)
    j = pl.program_id(1)
    x = n_ref[...]
    bm = x.shape[0]
    b_gate = jnp.dot(x, wb_ref[...], preferred_element_type=F32)
    c_gate = jnp.dot(x, wc_ref[...], preferred_element_type=F32)
    v = jnp.dot(x, wv_ref[...], preferred_element_type=F32)
    u = c_gate * v
    prev = jnp.where((i * bm) % seq == 0, 0.0, tail_ref[j])
    tail_ref[j] = u[bm - SUBLANES:, :]
    u1 = pltpu.roll(u, 1, axis=0)
    u2 = pltpu.roll(u, 2, axis=0)
    row = lax.broadcasted_iota(I32, prev.shape, 0)
    head1 = jnp.where(row < 1, pltpu.roll(prev, 1, axis=0), u1[:SUBLANES])
    head2 = jnp.where(row < 2, pltpu.roll(prev, 2, axis=0), u2[:SUBLANES])
    u1 = jnp.concatenate([head1, u1[SUBLANES:]], axis=0)
    u2 = jnp.concatenate([head2, u2[SUBLANES:]], axis=0)
    cw = cw_ref[...]
    conv = cw[0:1, :] * u2 + cw[1:2, :] * u1 + cw[2:3, :] * u
    y_ref[...] = (b_gate * conv).astype(BF16)


def _conv_in(n, w_in, conv_w, seq):
    m, d = n.shape
    bm = _tile(seq, 1024)
    bn = _tile(d, 256)
    nj = d // bn
    return pl.pallas_call(
        functools.partial(_conv_in_kernel, seq=seq),
        grid=(m // bm, nj),
        in_specs=[
            pl.BlockSpec((bm, d), lambda i, j: (i, 0)),
            pl.BlockSpec((d, bn), lambda i, j: (0, j)),
            pl.BlockSpec((d, bn), lambda i, j: (0, nj + j)),
            pl.BlockSpec((d, bn), lambda i, j: (0, 2 * nj + j)),
            pl.BlockSpec((CONV_W, bn), lambda i, j: (0, j)),
        ],
        out_specs=pl.BlockSpec((bm, bn), lambda i, j: (i, j)),
        out_shape=jax.ShapeDtypeStruct((m, d), BF16),
        scratch_shapes=[pltpu.VMEM((nj, SUBLANES, bn), F32)],
        compiler_params=_params(("arbitrary", "arbitrary"), 48),
        name="conv_in",
    )(n, w_in, w_in, w_in, conv_w)


def _matmul_res_kernel(a_ref, w_ref, h_ref, g_ref, o_ref):
    acc = jnp.dot(a_ref[...], w_ref[...], preferred_element_type=F32)
    o_ref[...] = h_ref[...] + g_ref[0] * acc


def _matmul_res(a, w, h, gate, seq):
    m, k = a.shape
    n = w.shape[1]
    bsz = gate.shape[0]
    bm = _tile(seq, 1024)
    bn = _tile(n, 512)
    return pl.pallas_call(
        _matmul_res_kernel,
        grid=(m // bm, n // bn),
        in_specs=[
            pl.BlockSpec((bm, k), lambda i, j: (i, 0)),
            pl.BlockSpec((k, bn), lambda i, j: (0, j)),
            pl.BlockSpec((bm, bn), lambda i, j: (i, j)),
            pl.BlockSpec((1, 1, bn), lambda i, j: ((i * bm) // seq, 0, j)),
        ],
        out_specs=pl.BlockSpec((bm, bn), lambda i, j: (i, j)),
        out_shape=jax.ShapeDtypeStruct((m, n), F32),
        compiler_params=_params(("arbitrary", "arbitrary"), 48),
        name="matmul_res",
    )(a, w, h, gate.reshape(bsz, 1, n))


def _group_sumsq(t, bd_ref):
    tt = t * t
    hi = tt.astype(BF16)
    lo = (tt - hi.astype(F32)).astype(BF16)
    bd = bd_ref[...]
    return jnp.dot(hi, bd, preferred_element_type=F32) + jnp.dot(lo, bd, preferred_element_type=F32)


def _kv_kernel(n_ref, w_ref, gain_ref, bd_ref, k_ref, v_ref, *, head_dim):
    kv = jnp.dot(n_ref[...], w_ref[...], preferred_element_type=F32)
    kvd = kv.shape[1] // 2
    k = kv[:, :kvd]
    v = kv[:, kvd:]
    ssq = _group_sumsq(k, bd_ref)
    kn = k * lax.rsqrt(ssq * (1.0 / head_dim) + QK_EPS) * gain_ref[...]
    for g in range(kvd // head_dim):
        k_ref[g] = kn[:, g * head_dim:(g + 1) * head_dim].astype(BF16)
        v_ref[g] = v[:, g * head_dim:(g + 1) * head_dim].astype(BF16)


def _kv_proj(n, w_kv, k_gain, bd, head_dim, seq):
    m, d = n.shape
    kvd = w_kv.shape[1] // 2
    nkv = kvd // head_dim
    bm = _tile(seq, 512)
    out = jax.ShapeDtypeStruct((nkv, m, head_dim), BF16)
    return pl.pallas_call(
        functools.partial(_kv_kernel, head_dim=head_dim),
        grid=(m // bm,),
        in_specs=[
            pl.BlockSpec((bm, d), lambda i: (i, 0)),
            pl.BlockSpec((d, 2 * kvd), lambda i: (0, 0)),
            pl.BlockSpec((1, kvd), lambda i: (0, 0)),
            pl.BlockSpec((kvd, kvd), lambda i: (0, 0)),
        ],
        out_specs=[pl.BlockSpec((nkv, bm, head_dim), lambda i: (0, i, 0))] * 2,
        out_shape=[out, out],
        compiler_params=_params(("arbitrary",), 48),
        name="kv_proj",
    )(n, w_kv, k_gain, bd)


def _q_attn_kernel(n_ref, wq_ref, gain_ref, bd_ref, kc_ref, kp_ref, vc_ref, vp_ref, bias_ref, sink_ref,
                   o_ref, *, seq, head_dim):
    i = pl.program_id(0)
    bm = n_ref.shape[0]
    group = wq_ref.shape[1] // head_dim
    q = jnp.dot(n_ref[...], wq_ref[...], preferred_element_type=F32)
    ssq = _group_sumsq(q, bd_ref)
    qn = (q * lax.rsqrt(ssq * (1.0 / head_dim) + QK_EPS) * gain_ref[...]).astype(BF16)
    bias = bias_ref[0]
    sink = sink_ref[0]
    col = lax.broadcasted_iota(I32, bias.shape, 1)
    for qb in range(bm // WINDOW):
        r0 = qb * WINDOW
        qs = jnp.concatenate(
            [qn[r0:r0 + WINDOW, hh * head_dim:(hh + 1) * head_dim] for hh in range(group)], axis=0)
        if qb == 0:
            k_prev, v_prev = kp_ref[0], vp_ref[0]
        else:
            k_prev, v_prev = kc_ref[0, r0 - WINDOW:r0, :], vc_ref[0, r0 - WINDOW:r0, :]
        k2 = jnp.concatenate([k_prev, kc_ref[0, r0:r0 + WINDOW, :]], axis=0)
        v2 = jnp.concatenate([v_prev, vc_ref[0, r0:r0 + WINDOW, :]], axis=0)
        s = lax.dot_general(qs, k2, (((1,), (1,)), ((), ())), preferred_element_type=F32) + bias
        first = (i * bm + r0) % seq == 0
        s = jnp.where(jnp.logical_and(first, col < WINDOW), -jnp.inf, s)
        mx = jnp.maximum(jnp.max(s, axis=-1, keepdims=True), sink)
        p = jnp.exp(s - mx)
        denom = jnp.sum(p, axis=-1, keepdims=True) + jnp.exp(sink - mx)
        o = jnp.dot(p.astype(BF16), v2, preferred_element_type=F32) / denom
        o_ref[r0:r0 + WINDOW, :] = jnp.concatenate(
            [o[hh * WINDOW:(hh + 1) * WINDOW, :] for hh in range(group)], axis=1).astype(BF16)


def _q_attn(n, w_q, q_gain, bd, k, v, bias, sink, head_dim, seq):
    m, d = n.shape
    nkv = k.shape[0]
    gw = w_q.shape[1] // nkv
    group = gw // head_dim
    bm = _tile(seq, 512)
    nb = bm // WINDOW
    prev_blk = lambda i, g: (g, jnp.maximum(i * nb - 1, 0), 0)
    return pl.pallas_call(
        functools.partial(_q_attn_kernel, seq=seq, head_dim=head_dim),
        grid=(m // bm, nkv),
        in_specs=[
            pl.BlockSpec((bm, d), lambda i, g: (i, 0)),
            pl.BlockSpec((d, gw), lambda i, g: (0, g)),
            pl.BlockSpec((1, gw), lambda i, g: (0, 0)),
            pl.BlockSpec((gw, gw), lambda i, g: (0, 0)),
            pl.BlockSpec((1, bm, head_dim), lambda i, g: (g, i, 0)),
            pl.BlockSpec((1, WINDOW, head_dim), prev_blk),
            pl.BlockSpec((1, bm, head_dim), lambda i, g: (g, i, 0)),
            pl.BlockSpec((1, WINDOW, head_dim), prev_blk),
            pl.BlockSpec((1, group * WINDOW, 2 * WINDOW), lambda i, g: (g, 0, 0)),
            pl.BlockSpec((1, group * WINDOW, 1), lambda i, g: (g, 0, 0)),
        ],
        out_specs=pl.BlockSpec((bm, gw), lambda i, g: (i, g)),
        out_shape=jax.ShapeDtypeStruct((m, w_q.shape[1]), BF16),
        compiler_params=_params(("arbitrary", "arbitrary"), 48),
        name="q_attn",
    )(n, w_q, q_gain, bd, k, k, v, v, bias, sink)


def _attn_bias(n_heads, nkv):
    h = np.arange(1, n_heads + 1, dtype=np.float32)
    slopes = np.power(2.0, -8.0 * h / n_heads).astype(np.float32)
    qi = np.arange(WINDOW)[:, None]
    kj = np.arange(2 * WINDOW)[None, :]
    dist = WINDOW + qi - kj
    ok = (dist >= 0) & (dist < WINDOW)
    alibi = -slopes[:, None, None] * dist.astype(np.float32)[None]
    bias = np.where(ok[None], alibi, -np.inf).astype(np.float32)
    return jnp.asarray(bias.reshape(nkv, (n_heads // nkv) * WINDOW, 2 * WINDOW))


def _dispatch_kernel(pos_ref, n_hbm, xs_in, xs_out, sem, *, bt):
    del xs_in
    i = pl.program_id(0)

    def issue(r, carry):
        t = i * bt + r
        for k in range(TOP_K):
            p = pos_ref[0, 0, r * TOP_K + k]
            pltpu.make_async_copy(n_hbm.at[pl.ds(t, 1)], xs_out.at[pl.ds(p, 1)], sem).start()
        return carry

    lax.fori_loop(0, bt, issue, 0)
    pltpu.make_async_copy(n_hbm.at[pl.ds(0, bt * TOP_K)], xs_out.at[pl.ds(0, bt * TOP_K)], sem).wait()


def _dispatch(pos, n_packed, xs):
    m, dh = n_packed.shape
    bt = _tile(m, 512)
    return pl.pallas_call(
        functools.partial(_dispatch_kernel, bt=bt),
        grid=(m // bt,),
        in_specs=[
            pl.BlockSpec((1, 1, bt * TOP_K), lambda i: (i, 0, 0), memory_space=pltpu.SMEM),
            pl.BlockSpec(memory_space=pl.ANY),
            pl.BlockSpec(memory_space=pl.ANY),
        ],
        out_specs=pl.BlockSpec(memory_space=pl.ANY),
        out_shape=jax.ShapeDtypeStruct(xs.shape, xs.dtype),
        scratch_shapes=[pltpu.SemaphoreType.DMA(())],
        input_output_aliases={2: 0},
        compiler_params=_params(("arbitrary",), 16),
        name="moe_dispatch",
    )(pos.reshape(m // bt, 1, bt * TOP_K), n_packed, xs)


def _moe_kernel(te_ref, tb_ref, nv_ref, xs_ref, wgu_ref, bgu_ref, wd_ref, bd_ref, ys_ref):
    del te_ref, tb_ref
    valid = pl.program_id(0) < nv_ref[0]

    @pl.when(jnp.logical_not(valid))
    def _():
        ys_ref[...] = jnp.zeros_like(ys_ref)

    @pl.when(valid)
    def _():
        lo, hi = _unpack_bf16_pair(xs_ref[...])
        half = lo.shape[1]
        f = wd_ref.shape[1]
        gu = (jnp.dot(lo.astype(BF16), wgu_ref[0, :half, :], preferred_element_type=F32)
              + jnp.dot(hi.astype(BF16), wgu_ref[0, half:, :], preferred_element_type=F32)
              + bgu_ref[0])
        x_glu = jnp.minimum(gu[:, :f], SWIGLU_LIMIT)
        x_lin = jnp.clip(gu[:, f:], -SWIGLU_LIMIT, SWIGLU_LIMIT)
        hid = x_glu * jax.nn.sigmoid(SWIGLU_ALPHA * x_glu) * (x_lin + 1.0)
        y = jnp.dot(hid.astype(BF16), wd_ref[0], preferred_element_type=F32) + bd_ref[0]
        ys_ref[...] = _pack_bf16_pair(y)


def _moe_grouped(tile_expert, tile_block, n_valid, xs, w_gu, b_gu, w_d, b_d, tm):
    rows, dh = xs.shape
    ne, d, f2 = w_gu.shape
    f = f2 // 2
    grid_spec = pltpu.PrefetchScalarGridSpec(
        num_scalar_prefetch=3,
        grid=(rows // tm,),
        in_specs=[
            pl.BlockSpec((tm, dh), lambda j, te, tb, nv: (tb[j], 0)),
            pl.BlockSpec((1, d, f2), lambda j, te, tb, nv: (te[j], 0, 0)),
            pl.BlockSpec((1, 1, f2), lambda j, te, tb, nv: (te[j], 0, 0)),
            pl.BlockSpec((1, f, d), lambda j, te, tb, nv: (te[j], 0, 0)),
            pl.BlockSpec((1, 1, d), lambda j, te, tb, nv: (te[j], 0, 0)),
        ],
        out_specs=pl.BlockSpec((tm, dh), lambda j, te, tb, nv: (tb[j], 0)),
    )
    return pl.pallas_call(
        _moe_kernel,
        grid_spec=grid_spec,
        out_shape=jax.ShapeDtypeStruct((rows, dh), U32),
        compiler_params=_params(("arbitrary",), 48),
        name="moe_grouped",
    )(tile_expert, tile_block, n_valid, xs, w_gu, b_gu.reshape(ne, 1, f2), w_d, b_d.reshape(ne, 1, d))


def _combine_kernel(pos_ref, h_ref, wt_ref, g_ref, ys_hbm, o_ref, buf, sem, *, bc):
    def issue(r, carry):
        for k in range(TOP_K):
            p = pos_ref[0, 0, r * TOP_K + k]
            pltpu.make_async_copy(ys_hbm.at[pl.ds(p, 1)], buf.at[k, pl.ds(r, 1)], sem).start()
        return carry

    lax.fori_loop(0, bc, issue, 0)
    for k in range(TOP_K):
        pltpu.make_async_copy(ys_hbm.at[pl.ds(0, bc)], buf.at[k], sem).wait()
    half = buf.shape[2]
    wt = wt_ref[...]
    acc_lo = jnp.zeros((bc, half), F32)
    acc_hi = jnp.zeros((bc, half), F32)
    for k in range(TOP_K):
        lo, hi = _unpack_bf16_pair(buf[k])
        wk = wt[:, k:k + 1]
        acc_lo = acc_lo + wk * lo
        acc_hi = acc_hi + wk * hi
    g = g_ref[0]
    o_ref[:, :half] = h_ref[:, :half] + g[:, :half] * acc_lo
    o_ref[:, half:] = h_ref[:, half:] + g[:, half:] * acc_hi


def _combine(pos, h, wt, gate, ys, seq):
    m, d = h.shape
    bsz = gate.shape[0]
    bc = _tile(seq, 256)
    return pl.pallas_call(
        functools.partial(_combine_kernel, bc=bc),
        grid=(m // bc,),
        in_specs=[
            pl.BlockSpec((1, 1, bc * TOP_K), lambda i: (i, 0, 0), memory_space=pltpu.SMEM),
            pl.BlockSpec((bc, d), lambda i: (i, 0)),
            pl.BlockSpec((bc, LANES), lambda i: (i, 0)),
            pl.BlockSpec((1, 1, d), lambda i: ((i * bc) // seq, 0, 0)),
            pl.BlockSpec(memory_space=pl.ANY),
        ],
        out_specs=pl.BlockSpec((bc, d), lambda i: (i, 0)),
        out_shape=jax.ShapeDtypeStruct((m, d), F32),
        scratch_shapes=[pltpu.VMEM((TOP_K, bc, d // 2), U32), pltpu.SemaphoreType.DMA(())],
        compiler_params=_params(("arbitrary",), 48),
        name="moe_combine",
    )(pos.reshape(m // bc, 1, bc * TOP_K), h, wt, gate.reshape(bsz, 1, d), ys)


def _moe_layer(h, gain, scale, shift, gate, rw, rb, w_gu, b_gu, w_d, b_d, xs, seq, tm):
    m, d = h.shape
    ne = rw.shape[1]
    rw_p = jnp.zeros((d, LANES), BF16).at[:, :ne].set(rw.astype(BF16))
    rb_p = jnp.full((1, LANES), NEG_BIG, F32).at[0, :ne].set(rb)
    n_packed, route, wt, counts = _norm_router(h, gain, scale, shift, rw_p, rb_p, seq)

    cnt = counts[0, :ne].astype(I32)
    padded = ((cnt + tm - 1) // tm) * tm
    ends = jnp.cumsum(padded)
    starts = ends - padded
    pos = starts[route[:, :TOP_K]] + route[:, TOP_K:2 * TOP_K]
    n_tiles = xs.shape[0] // tm
    n_valid = ends[-1] // tm
    tile_id = jnp.arange(n_tiles, dtype=I32)
    tile_expert = jnp.searchsorted(ends, jnp.minimum(tile_id, n_valid - 1) * tm, side="right").astype(I32)

    xs = _dispatch(pos, n_packed, xs)
    ys = _moe_grouped(tile_expert, tile_id, n_valid.reshape(1).astype(I32), xs, w_gu, b_gu, w_d, b_d, tm)
    return _combine(pos, h, wt, gate, ys, seq), xs


def kernel(x, c, norm_mix, norm_ffn, mod_down, mod_up, mod_b, a_w_in, a_conv, a_w_out, kv_norm, kv_mod_down, kv_mod_up, kv_mod_b, w_kv, k_norm, b_w_q, b_q_norm, b_sinks, b_w_o, router_w, router_b, moe_w_gu, moe_b_gu, moe_w_down, moe_b_down):
    bsz, seq, d = x.shape
    m = bsz * seq
    depth = norm_mix.shape[0]
    n_a = a_w_in.shape[0]
    head_dim = k_norm.shape[0]
    n_heads = b_sinks.shape[1]
    nkv = w_kv.shape[1] // (2 * head_dim)
    group = n_heads // nkv
    ne = router_w.shape[2]
    tm = 256
    assert seq % WINDOW == 0 and (m * TOP_K) % tm == 0

    mods = _adaln(c, mod_down, mod_up, mod_b)
    kv_mods = _adaln(c, kv_mod_down[None], kv_mod_up[None], kv_mod_b[None])[0]

    gw = group * head_dim
    bd = jnp.asarray(np.kron(np.eye(gw // head_dim, dtype=np.float32),
                             np.ones((head_dim, head_dim), np.float32)), BF16)
    kvd = nkv * head_dim
    bd_kv = jnp.asarray(np.kron(np.eye(nkv, dtype=np.float32),
                                np.ones((head_dim, head_dim), np.float32)), BF16)
    attn_bias = _attn_bias(n_heads, nkv)

    h = x.reshape(m, d)
    xs = jnp.zeros((m * TOP_K + ne * tm, d // 2), U32)
    k = v = None
    for layer in range(depth):
        sh_m, sc_m, g_m, sh_f, sc_f, g_f = jnp.split(mods[layer], 6, axis=-1)
        n = _norm_mod(h, norm_mix[layer], sc_m, sh_m, seq)
        if layer < n_a:
            y = _conv_in(n, a_w_in[layer].astype(BF16), a_conv[layer], seq)
            h = _matmul_res(y, a_w_out[layer].astype(BF16), h, g_m, seq)
        else:
            if layer == n_a:
                sh_kv, sc_kv = jnp.split(kv_mods, 2, axis=-1)
                n_kv = _norm_mod(h, kv_norm, sc_kv, sh_kv, seq)
                k, v = _kv_proj(n_kv, w_kv.astype(BF16), jnp.tile(k_norm, nkv).reshape(1, kvd), bd_kv,
                                head_dim, seq)
            i = layer - n_a
            q_gain = (jnp.tile(b_q_norm[i], group) * head_dim ** -0.5).reshape(1, gw)
            sink = jnp.repeat(b_sinks[i].reshape(nkv, group), WINDOW, axis=1).reshape(nkv, group * WINDOW, 1)
            a = _q_attn(n, b_w_q[i].astype(BF16), q_gain, bd, k, v, attn_bias, sink, head_dim, seq)
            h = _matmul_res(a, b_w_o[i].astype(BF16), h, g_m, seq)
        h, xs = _moe_layer(h, norm_ffn[layer], sc_f, sh_f, g_f, router_w[layer], router_b[layer],
                           moe_w_gu[layer].astype(BF16), moe_b_gu[layer], moe_w_down[layer].astype(BF16),
                           moe_b_down[layer], xs, seq, tm)
    return h.reshape(bsz, seq, d)
```

```python
import functools

import numpy as np
import jax
import jax.numpy as jnp
from jax import lax
from jax.experimental import pallas as pl
from jax.experimental.pallas import tpu as pltpu

WINDOW = 128
TOP_K = 4
CONV_W = 3
NORM_EPS = 1e-5
QK_EPS = 1e-6
SWIGLU_ALPHA = 1.702
SWIGLU_LIMIT = 7.0
LANES = 128
SUBLANES = 8
NEG_BIG = -1e30
MOE_OUT_CHUNKS = 4
MIB = 1024 * 1024

F32 = jnp.float32
BF16 = jnp.bfloat16
U32 = jnp.uint32
I32 = jnp.int32
HI_MASK = np.uint32(0xFFFF0000)


def _tile(total, pref):
    t = min(total, pref)
    assert total % t == 0, (total, pref)
    return t


def _params(sem, vmem_mib):
    return pltpu.CompilerParams(dimension_semantics=sem, vmem_limit_bytes=vmem_mib * MIB)


def _pack_bf16_pair(a, b):
    lo = lax.bitcast_convert_type(a.astype(BF16).astype(F32), U32) >> 16
    hi = lax.bitcast_convert_type(b.astype(BF16).astype(F32), U32) & HI_MASK
    return lo | hi


def _unpack_bf16_pair(w):
    lo = lax.bitcast_convert_type(w << 16, F32)
    hi = lax.bitcast_convert_type(w & HI_MASK, F32)
    return lo, hi


def _adaln_kernel(c_ref, wd_ref, wu_ref, b_ref, o_ref, d_ref):
    @pl.when(pl.program_id(1) == 0)
    def _():
        c = c_ref[...]
        s = c * jax.nn.sigmoid(c)
        d_ref[...] = jnp.dot(s, wd_ref[0], preferred_element_type=F32, precision=lax.Precision.HIGHEST)

    o_ref[0] = jnp.dot(d_ref[...], wu_ref[0], preferred_element_type=F32,
                       precision=lax.Precision.HIGHEST) + b_ref[0]


def _adaln(c, w_down, w_up, b):
    nl, d, r = w_down.shape
    n = w_up.shape[2]
    bsz = c.shape[0]
    tn = _tile(n, min(d, 2048))
    return pl.pallas_call(
        _adaln_kernel,
        grid=(nl, n // tn),
        in_specs=[
            pl.BlockSpec((bsz, d), lambda l, j: (0, 0)),
            pl.BlockSpec((1, d, r), lambda l, j: (l, 0, 0)),
            pl.BlockSpec((1, r, tn), lambda l, j: (l, 0, j)),
            pl.BlockSpec((1, 1, tn), lambda l, j: (l, 0, j)),
        ],
        out_specs=pl.BlockSpec((1, bsz, tn), lambda l, j: (l, 0, j)),
        out_shape=jax.ShapeDtypeStruct((nl, bsz, n), F32),
        scratch_shapes=[pltpu.VMEM((bsz, r), F32)],
        compiler_params=_params(("arbitrary", "arbitrary"), 32),
        name="adaln",
    )(c, w_down, w_up, b.reshape(nl, 1, n))


def _norm_mod_value(h_ref, g_ref, sc_ref, sh_ref):
    x = h_ref[...]
    ms = jnp.mean(x * x, axis=-1, keepdims=True)
    y = x * lax.rsqrt(ms + NORM_EPS) * g_ref[...]
    return y * (1.0 + sc_ref[0]) + sh_ref[0]


def _norm_kernel(h_ref, g_ref, sc_ref, sh_ref, o_ref):
    o_ref[...] = _norm_mod_value(h_ref, g_ref, sc_ref, sh_ref).astype(BF16)


def _norm_mod(h, gain, scale, shift, seq):
    m, d = h.shape
    bsz = scale.shape[0]
    bm = _tile(seq, 256)
    row_b = lambda i: ((i * bm) // seq, 0, 0)
    return pl.pallas_call(
        _norm_kernel,
        grid=(m // bm,),
        in_specs=[
            pl.BlockSpec((bm, d), lambda i: (i, 0)),
            pl.BlockSpec((1, d), lambda i: (0, 0)),
            pl.BlockSpec((1, 1, d), row_b),
            pl.BlockSpec((1, 1, d), row_b),
        ],
        out_specs=pl.BlockSpec((bm, d), lambda i: (i, 0)),
        out_shape=jax.ShapeDtypeStruct((m, d), BF16),
        compiler_params=_params(("arbitrary",), 40),
        name="norm_mod",
    )(h, gain.reshape(1, d), scale.reshape(bsz, 1, d), shift.reshape(bsz, 1, d))


def _norm_router_kernel(h_ref, g_ref, sc_ref, sh_ref, rw_ref, rb_ref,
                        np_ref, ri_ref, wt_ref, cnt_ref, carry_ref):
    @pl.when(pl.program_id(0) == 0)
    def _():
        carry_ref[...] = jnp.zeros_like(carry_ref)

    y = _norm_mod_value(h_ref, g_ref, sc_ref, sh_ref)
    half = y.shape[1] // 2
    np_ref[...] = _pack_bf16_pair(y[:, :half], y[:, half:])
    logits = jnp.dot(y.astype(BF16), rw_ref[...], preferred_element_type=F32) + rb_ref[...]
    bm = logits.shape[0]
    lane = lax.broadcasted_iota(I32, logits.shape, 1)

    l = logits
    sels, vals, idxs = [], [], []
    for _ in range(TOP_K):
        mx = jnp.max(l, axis=-1, keepdims=True)
        idx = jnp.min(jnp.where(l == mx, lane, LANES), axis=-1, keepdims=True)
        sel = lane == idx
        l = jnp.where(sel, -jnp.inf, l)
        sels.append(sel)
        vals.append(mx)
        idxs.append(idx)
    exps = [jnp.exp(v - vals[0]) for v in vals]
    denom = exps[0]
    for e in exps[1:]:
        denom = denom + e

    mask = sels[0].astype(F32)
    for sel in sels[1:]:
        mask = mask + sel.astype(F32)
    rr = lax.broadcasted_iota(I32, (bm, bm), 0)
    cc = lax.broadcasted_iota(I32, (bm, bm), 1)
    tri = (cc < rr).astype(BF16)
    carry = carry_ref[0:1, :]
    prefix = jnp.dot(tri, mask.astype(BF16), preferred_element_type=F32) + carry

    ri = jnp.zeros(logits.shape, I32)
    wt = jnp.zeros(logits.shape, F32)
    for k in range(TOP_K):
        rank = jnp.sum(jnp.where(sels[k], prefix, 0.0), axis=-1, keepdims=True).astype(I32)
        ri = jnp.where(lane == k, idxs[k], ri)
        ri = jnp.where(lane == TOP_K + k, rank, ri)
        wt = jnp.where(lane == k, exps[k] / denom, wt)
    ri_ref[...] = ri
    wt_ref[...] = wt
    new_carry = jnp.broadcast_to(carry + jnp.sum(mask, axis=0, keepdims=True), carry_ref.shape)
    carry_ref[...] = new_carry
    cnt_ref[...] = new_carry


def _norm_router(h, gain, scale, shift, rw, rb, layer, seq):
    m, d = h.shape
    bsz = scale.shape[0]
    bm = _tile(seq, 256)
    row_b = lambda i: ((i * bm) // seq, 0, 0)
    return pl.pallas_call(
        _norm_router_kernel,
        grid=(m // bm,),
        in_specs=[
            pl.BlockSpec((bm, d), lambda i: (i, 0)),
            pl.BlockSpec((1, d), lambda i: (0, 0)),
            pl.BlockSpec((1, 1, d), row_b),
            pl.BlockSpec((1, 1, d), row_b),
            pl.BlockSpec((None, d, LANES), lambda i: (layer, 0, 0)),
            pl.BlockSpec((None, 1, LANES), lambda i: (layer, 0, 0)),
        ],
        out_specs=[
            pl.BlockSpec((bm, d // 2), lambda i: (i, 0)),
            pl.BlockSpec((bm, LANES), lambda i: (i, 0)),
            pl.BlockSpec((bm, LANES), lambda i: (i, 0)),
            pl.BlockSpec((SUBLANES, LANES), lambda i: (0, 0)),
        ],
        out_shape=[
            jax.ShapeDtypeStruct((m, d // 2), U32),
            jax.ShapeDtypeStruct((m, LANES), I32),
            jax.ShapeDtypeStruct((m, LANES), F32),
            jax.ShapeDtypeStruct((SUBLANES, LANES), F32),
        ],
        scratch_shapes=[pltpu.VMEM((SUBLANES, LANES), F32)],
        compiler_params=_params(("arbitrary",), 40),
        name="norm_router",
    )(h, gain.reshape(1, d), scale.reshape(bsz, 1, d), shift.reshape(bsz, 1, d), rw, rb)


def _conv_in_kernel(n_ref, wb_ref, wc_ref, wv_ref, cw_ref, y_ref, tail_ref, *, seq):
    i = pl.program_id(0)
    j = pl.program_id(1)
    x = n_ref[...]
    bm = x.shape[0]
    b_gate = jnp.dot(x, wb_ref[...], preferred_element_type=F32)
    c_gate = jnp.dot(x, wc_ref[...], preferred_element_type=F32)
    v = jnp.dot(x, wv_ref[...], preferred_element_type=F32)
    u = c_gate * v
    prev = jnp.where((i * bm) % seq == 0, 0.0, tail_ref[j])
    tail_ref[j] = u[bm - SUBLANES:, :]
    u1 = pltpu.roll(u, 1, axis=0)
    u2 = pltpu.roll(u, 2, axis=0)
    row = lax.broadcasted_iota(I32, prev.shape, 0)
    head1 = jnp.where(row < 1, pltpu.roll(prev, 1, axis=0), u1[:SUBLANES])
    head2 = jnp.where(row < 2, pltpu.roll(prev, 2, axis=0), u2[:SUBLANES])
    u1 = jnp.concatenate([head1, u1[SUBLANES:]], axis=0)
    u2 = jnp.concatenate([head2, u2[SUBLANES:]], axis=0)
    cw = cw_ref[...]
    conv = cw[0:1, :] * u2 + cw[1:2, :] * u1 + cw[2:3, :] * u
    y_ref[...] = (b_gate * conv).astype(BF16)


def _conv_in(n, w_in, conv_w, layer, seq):
    m, d = n.shape
    bm = _tile(seq, 1024)
    bn = _tile(d, 256)
    nj = d // bn
    return pl.pallas_call(
        functools.partial(_conv_in_kernel, seq=seq),
        grid=(m // bm, nj),
        in_specs=[
            pl.BlockSpec((bm, d), lambda i, j: (i, 0)),
            pl.BlockSpec((None, d, bn), lambda i, j: (layer, 0, j)),
            pl.BlockSpec((None, d, bn), lambda i, j: (layer, 0, nj + j)),
            pl.BlockSpec((None, d, bn), lambda i, j: (layer, 0, 2 * nj + j)),
            pl.BlockSpec((None, CONV_W, bn), lambda i, j: (layer, 0, j)),
        ],
        out_specs=pl.BlockSpec((bm, bn), lambda i, j: (i, j)),
        out_shape=jax.ShapeDtypeStruct((m, d), BF16),
        scratch_shapes=[pltpu.VMEM((nj, SUBLANES, bn), F32)],
        compiler_params=_params(("arbitrary", "arbitrary"), 48),
        name="conv_in",
    )(n, w_in, w_in, w_in, conv_w)


def _matmul_res_kernel(a_ref, w_ref, h_ref, g_ref, o_ref):
    acc = jnp.dot(a_ref[...], w_ref[...], preferred_element_type=F32)
    o_ref[...] = h_ref[...] + g_ref[0] * acc


def _matmul_res(a, w, h, gate, layer, seq):
    m, k = a.shape
    n = w.shape[2]
    bsz = gate.shape[0]
    bm = _tile(seq, 1024)
    bn = _tile(n, 512)
    return pl.pallas_call(
        _matmul_res_kernel,
        grid=(m // bm, n // bn),
        in_specs=[
            pl.BlockSpec((bm, k), lambda i, j: (i, 0)),
            pl.BlockSpec((None, k, bn), lambda i, j: (layer, 0, j)),
            pl.BlockSpec((bm, bn), lambda i, j: (i, j)),
            pl.BlockSpec((1, 1, bn), lambda i, j: ((i * bm) // seq, 0, j)),
        ],
        out_specs=pl.BlockSpec((bm, bn), lambda i, j: (i, j)),
        out_shape=jax.ShapeDtypeStruct((m, n), F32),
        compiler_params=_params(("arbitrary", "arbitrary"), 48),
        name="matmul_res",
    )(a, w, h, gate.reshape(bsz, 1, n))


def _group_sumsq(t, bd_ref):
    tt = t * t
    hi = tt.astype(BF16)
    lo = (tt - hi.astype(F32)).astype(BF16)
    bd = bd_ref[...]
    return jnp.dot(hi, bd, preferred_element_type=F32) + jnp.dot(lo, bd, preferred_element_type=F32)


def _kv_kernel(n_ref, wk_ref, wvt_ref, gain_ref, bd_ref, k_ref, vt_ref, *, head_dim):
    x = n_ref[...]
    k = jnp.dot(x, wk_ref[...], preferred_element_type=F32)
    ssq = _group_sumsq(k, bd_ref)
    kn = k * lax.rsqrt(ssq * (1.0 / head_dim) + QK_EPS) * gain_ref[...]
    for g in range(k.shape[1] // head_dim):
        k_ref[g] = kn[:, g * head_dim:(g + 1) * head_dim].astype(BF16)
    vt_ref[...] = lax.dot_general(wvt_ref[...], x, (((1,), (1,)), ((), ())),
                                  preferred_element_type=F32).astype(BF16)


def _kv_proj(n, w_k, w_vt, k_gain, bd, head_dim, seq):
    m, d = n.shape
    kvd = w_k.shape[1]
    nkv = kvd // head_dim
    bm = _tile(seq, 512)
    return pl.pallas_call(
        functools.partial(_kv_kernel, head_dim=head_dim),
        grid=(m // bm,),
        in_specs=[
            pl.BlockSpec((bm, d), lambda i: (i, 0)),
            pl.BlockSpec((d, kvd), lambda i: (0, 0)),
            pl.BlockSpec((kvd, d), lambda i: (0, 0)),
            pl.BlockSpec((1, kvd), lambda i: (0, 0)),
            pl.BlockSpec((kvd, kvd), lambda i: (0, 0)),
        ],
        out_specs=[pl.BlockSpec((nkv, bm, head_dim), lambda i: (0, i, 0)),
                   pl.BlockSpec((kvd, bm), lambda i: (0, i))],
        out_shape=[jax.ShapeDtypeStruct((nkv, m, head_dim), BF16),
                   jax.ShapeDtypeStruct((kvd, m), BF16)],
        compiler_params=_params(("arbitrary",), 48),
        name="kv_proj",
    )(n, w_k, w_vt, k_gain, bd)


def _q_attn_kernel(n_ref, wqt_ref, gain_ref, kc_ref, kp_ref, vc_ref, vp_ref, bias_ref, sink_ref,
                   o_ref, *, seq, head_dim):
    i = pl.program_id(0)
    bm = n_ref.shape[0]
    group = wqt_ref.shape[0] // head_dim
    qt = lax.dot_general(wqt_ref[...], n_ref[...], (((1,), (1,)), ((), ())), preferred_element_type=F32)
    gain = gain_ref[...]
    qn = []
    for hh in range(group):
        qh = qt[hh * head_dim:(hh + 1) * head_dim, :]
        rs = lax.rsqrt(jnp.sum(qh * qh, axis=0, keepdims=True) * (1.0 / head_dim) + QK_EPS)
        qn.append((qh * rs * gain[hh * head_dim:(hh + 1) * head_dim, :]).astype(BF16))
    sink = sink_ref[...]
    shape = (WINDOW, group * WINDOW)
    key_c = lax.broadcasted_iota(I32, shape, 0)
    query_i = lax.broadcasted_iota(I32, shape, 1) & (WINDOW - 1)
    in_cur = key_c <= query_i
    for qb in range(bm // WINDOW):
        r0 = qb * WINDOW
        q_all = jnp.concatenate([q[:, r0:r0 + WINDOW] for q in qn], axis=1)
        if qb == 0:
            k_prev, vt_prev = kp_ref[0], vp_ref[...]
        else:
            k_prev, vt_prev = kc_ref[0, r0 - WINDOW:r0, :], vc_ref[:, r0 - WINDOW:r0]
        k2 = jnp.concatenate([k_prev, kc_ref[0, r0:r0 + WINDOW, :]], axis=0)
        v2t = jnp.concatenate([vt_prev, vc_ref[:, r0:r0 + WINDOW]], axis=1)
        st = jnp.dot(k2, q_all, preferred_element_type=F32)
        first = ((i * bm + r0) % seq == 0).astype(I32)
        s = jnp.where(in_cur, st[WINDOW:], st[:WINDOW]) + bias_ref[first]
        mx = jnp.maximum(jnp.max(s, axis=0, keepdims=True), sink)
        p = jnp.exp(s - mx)
        denom = jnp.sum(p, axis=0, keepdims=True) + jnp.exp(sink - mx)
        pt = jnp.concatenate([jnp.where(in_cur, 0.0, p), jnp.where(in_cur, p, 0.0)], axis=0).astype(BF16)
        ot = jnp.dot(v2t, pt, preferred_element_type=F32) * (1.0 / denom)
        for pr in range(group // 2):
            blk = jnp.concatenate([ot[:, (2 * pr) * WINDOW:(2 * pr + 1) * WINDOW],
                                   ot[:, (2 * pr + 1) * WINDOW:(2 * pr + 2) * WINDOW]], axis=0)
            o_ref[r0:r0 + WINDOW, pr * 2 * head_dim:(pr + 1) * 2 * head_dim] = blk.T.astype(BF16)


def _q_attn(n, w_qt, q_gain, k, vt, bias, sink, layer, head_dim, seq):
    m, d = n.shape
    nkv = k.shape[0]
    gw = w_qt.shape[1] // nkv
    group = gw // head_dim
    assert head_dim == WINDOW // 2 and group % 2 == 0
    bm = q_gain.shape[1]
    nb = bm // WINDOW
    return pl.pallas_call(
        functools.partial(_q_attn_kernel, seq=seq, head_dim=head_dim),
        grid=(m // bm, nkv),
        in_specs=[
            pl.BlockSpec((bm, d), lambda i, g: (i, 0)),
            pl.BlockSpec((None, gw, d), lambda i, g: (layer, g, 0)),
            pl.BlockSpec((gw, bm), lambda i, g: (0, 0)),
            pl.BlockSpec((1, bm, head_dim), lambda i, g: (g, i, 0)),
            pl.BlockSpec((1, WINDOW, head_dim), lambda i, g: (g, jnp.maximum(i * nb - 1, 0), 0)),
            pl.BlockSpec((head_dim, bm), lambda i, g: (g, i)),
            pl.BlockSpec((head_dim, WINDOW), lambda i, g: (g, jnp.maximum(i * nb - 1, 0))),
            pl.BlockSpec((None, 2, WINDOW, group * WINDOW), lambda i, g: (g, 0, 0, 0)),
            pl.BlockSpec((None, 1, group * WINDOW), lambda i, g: (g, 0, 0)),
        ],
        out_specs=pl.BlockSpec((bm, gw), lambda i, g: (i, g)),
        out_shape=jax.ShapeDtypeStruct((m, w_qt.shape[1]), BF16),
        compiler_params=_params(("arbitrary", "arbitrary"), 48),
        name="q_attn",
    )(n, w_qt, q_gain, k, k, vt, vt, bias, sink)


def _attn_bias(n_heads, nkv):
    group = n_heads // nkv
    h = np.arange(1, n_heads + 1, dtype=np.float32)
    slopes = np.power(2.0, -8.0 * h / n_heads).astype(np.float32)
    c = np.arange(WINDOW)[:, None]
    qi = np.arange(WINDOW)[None, :]
    in_cur = c <= qi
    dist = np.where(in_cur, qi - c, WINDOW + qi - c).astype(np.float32)
    alibi = -slopes[:, None, None] * dist[None]
    first = np.where(in_cur[None], alibi, -np.inf).astype(np.float32)
    tables = np.stack([alibi, first], axis=1)
    tables = tables.reshape(nkv, group, 2, WINDOW, WINDOW).transpose(0, 2, 3, 1, 4)
    return jnp.asarray(tables.reshape(nkv, 2, WINDOW, group * WINDOW))


def _dispatch_kernel(pos_ref, n_ref, xs_in, xs_out, sem, *, bt):
    def issue(r, carry):
        for k in range(TOP_K):
            p = pos_ref[0, 0, r * TOP_K + k]
            pltpu.make_async_copy(n_ref.at[pl.ds(r, 1)], xs_out.at[pl.ds(p, 1)], sem).start()
        return carry

    lax.fori_loop(0, bt, issue, 0)
    pltpu.make_async_copy(xs_in.at[pl.ds(0, bt * TOP_K)], xs_out.at[pl.ds(0, bt * TOP_K)], sem).wait()


def _dispatch(pos, n_packed, xs):
    m, dh = n_packed.shape
    bt = _tile(m, 512)
    return pl.pallas_call(
        functools.partial(_dispatch_kernel, bt=bt),
        grid=(m // bt,),
        in_specs=[
            pl.BlockSpec((1, 1, bt * TOP_K), lambda i: (i, 0, 0), memory_space=pltpu.SMEM),
            pl.BlockSpec((bt, dh), lambda i: (i, 0)),
            pl.BlockSpec(memory_space=pl.ANY),
        ],
        out_specs=pl.BlockSpec(memory_space=pl.ANY),
        out_shape=jax.ShapeDtypeStruct(xs.shape, xs.dtype),
        scratch_shapes=[pltpu.SemaphoreType.DMA(())],
        input_output_aliases={2: 0},
        compiler_params=_params(("arbitrary",), 24),
        name="moe_dispatch",
    )(pos.reshape(m // bt, 1, bt * TOP_K), n_packed, xs)


def _moe_kernel(te_ref, tb_ref, nv_ref, xs_ref, wgu_ref, bgu_ref, wd_ref, bd_ref, ys_ref):
    del te_ref, tb_ref
    valid = pl.program_id(0) < nv_ref[0]

    @pl.when(jnp.logical_not(valid))
    def _():
        ys_ref[...] = jnp.zeros_like(ys_ref)

    @pl.when(valid)
    def _():
        lo, hi = _unpack_bf16_pair(xs_ref[...])
        half = lo.shape[1]
        f = wd_ref.shape[1]
        x = jnp.concatenate([lo.astype(BF16), hi.astype(BF16)], axis=1)
        gu = jnp.dot(x, wgu_ref[0], preferred_element_type=F32) + bgu_ref[0]
        x_glu = jnp.minimum(gu[:, :f], SWIGLU_LIMIT)
        x_lin = jnp.clip(gu[:, f:], -SWIGLU_LIMIT, SWIGLU_LIMIT)
        hid = (x_glu * jax.nn.sigmoid(SWIGLU_ALPHA * x_glu) * (x_lin + 1.0)).astype(BF16)
        cw = half // MOE_OUT_CHUNKS
        for ch in range(MOE_OUT_CHUNKS):
            a0, b0 = ch * cw, half + ch * cw
            ya = jnp.dot(hid, wd_ref[0, :, a0:a0 + cw], preferred_element_type=F32) + bd_ref[0, :, a0:a0 + cw]
            yb = jnp.dot(hid, wd_ref[0, :, b0:b0 + cw], preferred_element_type=F32) + bd_ref[0, :, b0:b0 + cw]
            ys_ref[:, a0:a0 + cw] = _pack_bf16_pair(ya, yb)


def _moe_grouped(tile_expert, tile_block, n_valid, xs, w_gu, b_gu, w_d, b_d, layer, tm):
    rows, dh = xs.shape
    _, ne, d, f2 = w_gu.shape
    f = f2 // 2
    assert dh % MOE_OUT_CHUNKS == 0
    by_expert = lambda j, te, tb, nv: (layer, te[j], 0, 0)
    grid_spec = pltpu.PrefetchScalarGridSpec(
        num_scalar_prefetch=3,
        grid=(rows // tm,),
        in_specs=[
            pl.BlockSpec((tm, dh), lambda j, te, tb, nv: (tb[j], 0)),
            pl.BlockSpec((None, 1, d, f2), by_expert),
            pl.BlockSpec((None, 1, 1, f2), by_expert),
            pl.BlockSpec((None, 1, f, d), by_expert),
            pl.BlockSpec((None, 1, 1, d), by_expert),
        ],
        out_specs=pl.BlockSpec((tm, dh), lambda j, te, tb, nv: (tb[j], 0)),
    )
    return pl.pallas_call(
        _moe_kernel,
        grid_spec=grid_spec,
        out_shape=jax.ShapeDtypeStruct((rows, dh), U32),
        compiler_params=_params(("arbitrary",), 56),
        name="moe_grouped",
    )(tile_expert, tile_block, n_valid, xs, w_gu, b_gu, w_d, b_d)


def _combine_kernel(pos_ref, pos_next_ref, h_ref, wt_ref, g_ref, ys_hbm, o_ref, buf, sem, *, bc):
    i = pl.program_id(0)
    slot = i % 2

    def gather_rows(p_ref, s):
        def issue(r, carry):
            for k in range(TOP_K):
                p = p_ref[0, 0, r * TOP_K + k]
                pltpu.make_async_copy(ys_hbm.at[pl.ds(p, 1)], buf.at[s, k, pl.ds(r, 1)], sem.at[s]).start()
            return carry

        lax.fori_loop(0, bc, issue, 0)

    @pl.when(i == 0)
    def _():
        gather_rows(pos_ref, 0)

    @pl.when(i + 1 < pl.num_programs(0))
    def _():
        gather_rows(pos_next_ref, 1 - slot)

    for k in range(TOP_K):
        pltpu.make_async_copy(ys_hbm.at[pl.ds(0, bc)], buf.at[slot, k], sem.at[slot]).wait()
    half = buf.shape[3]
    wt = wt_ref[...]
    acc_lo = jnp.zeros((bc, half), F32)
    acc_hi = jnp.zeros((bc, half), F32)
    for k in range(TOP_K):
        lo, hi = _unpack_bf16_pair(buf[slot, k])
        wk = wt[:, k:k + 1]
        acc_lo = acc_lo + wk * lo
        acc_hi = acc_hi + wk * hi
    g = g_ref[0]
    o_ref[:, :half] = h_ref[:, :half] + g[:, :half] * acc_lo
    o_ref[:, half:] = h_ref[:, half:] + g[:, half:] * acc_hi


def _combine(pos, h, wt, gate, ys, seq):
    m, d = h.shape
    bsz = gate.shape[0]
    bc = _tile(seq, 256)
    nblk = m // bc
    pos_blocks = pos.reshape(nblk, 1, bc * TOP_K)
    return pl.pallas_call(
        functools.partial(_combine_kernel, bc=bc),
        grid=(nblk,),
        in_specs=[
            pl.BlockSpec((1, 1, bc * TOP_K), lambda i: (i, 0, 0), memory_space=pltpu.SMEM),
            pl.BlockSpec((1, 1, bc * TOP_K), lambda i: (jnp.minimum(i + 1, nblk - 1), 0, 0),
                         memory_space=pltpu.SMEM),
            pl.BlockSpec((bc, d), lambda i: (i, 0)),
            pl.BlockSpec((bc, LANES), lambda i: (i, 0)),
            pl.BlockSpec((1, 1, d), lambda i: ((i * bc) // seq, 0, 0)),
            pl.BlockSpec(memory_space=pl.ANY),
        ],
        out_specs=pl.BlockSpec((bc, d), lambda i: (i, 0)),
        out_shape=jax.ShapeDtypeStruct((m, d), F32),
        scratch_shapes=[pltpu.VMEM((2, TOP_K, bc, d // 2), U32), pltpu.SemaphoreType.DMA((2,))],
        compiler_params=_params(("arbitrary",), 48),
        name="moe_combine",
    )(pos_blocks, pos_blocks, h, wt, gate.reshape(bsz, 1, d), ys)


def _moe_layer(h, gain, scale, shift, gate, rw, rb, w_gu, b_gu, w_d, b_d, xs, layer, seq, tm):
    ne = w_gu.shape[1]
    n_packed, route, wt, counts = _norm_router(h, gain, scale, shift, rw, rb, layer, seq)

    cnt = counts[0, :ne].astype(I32)
    padded = ((cnt + tm - 1) // tm) * tm
    ends = jnp.cumsum(padded)
    starts = ends - padded
    expert_ids = jnp.arange(ne, dtype=I32)
    idx = route[:, :TOP_K]
    pos = jnp.sum(jnp.where(idx[:, :, None] == expert_ids, starts, 0), axis=-1) + route[:, TOP_K:2 * TOP_K]
    n_tiles = xs.shape[0] // tm
    n_valid = ends[-1] // tm
    tile_id = jnp.arange(n_tiles, dtype=I32)
    tile_start = jnp.minimum(tile_id, n_valid - 1) * tm
    tile_expert = jnp.sum((ends[None, :] <= tile_start[:, None]).astype(I32), axis=1)

    xs = _dispatch(pos, n_packed, xs)
    ys = _moe_grouped(tile_expert, tile_id, n_valid.reshape(1).astype(I32), xs, w_gu, b_gu, w_d, b_d, layer, tm)
    return _combine(pos, h, wt, gate, ys, seq), xs


def kernel(x, c, norm_mix, norm_ffn, mod_down, mod_up, mod_b, a_w_in, a_conv, a_w_out, kv_norm, kv_mod_down, kv_mod_up, kv_mod_b, w_kv, k_norm, b_w_q, b_q_norm, b_sinks, b_w_o, router_w, router_b, moe_w_gu, moe_b_gu, moe_w_down, moe_b_down):
    bsz, seq, d = x.shape
    m = bsz * seq
    depth = norm_mix.shape[0]
    n_a = a_w_in.shape[0]
    head_dim = k_norm.shape[0]
    n_heads = b_sinks.shape[1]
    nkv = w_kv.shape[1] // (2 * head_dim)
    group = n_heads // nkv
    ne = router_w.shape[2]
    tm = _tile(m * TOP_K, 512)
    attn_bm = _tile(seq, 512)
    assert seq % WINDOW == 0

    mods = _adaln(c, mod_down, mod_up, mod_b)
    kv_mods = _adaln(c, kv_mod_down[None], kv_mod_up[None], kv_mod_b[None])[0]

    w_in = a_w_in.astype(BF16)
    w_out = a_w_out.astype(BF16)
    w_qt = jnp.swapaxes(b_w_q, 1, 2).astype(BF16)
    w_o = b_w_o.astype(BF16)
    kvd = nkv * head_dim
    w_k = w_kv[:, :kvd].astype(BF16)
    w_vt = w_kv[:, kvd:].T.astype(BF16)
    w_gu = moe_w_gu.astype(BF16)
    w_d = moe_w_down.astype(BF16)
    b_gu = moe_b_gu[:, :, None, :]
    b_d = moe_b_down[:, :, None, :]
    rw = jnp.zeros((depth, d, LANES), BF16).at[:, :, :ne].set(router_w.astype(BF16))
    rb = jnp.full((depth, 1, LANES), NEG_BIG, F32).at[:, 0, :ne].set(router_b)

    gw = group * head_dim
    bd_kv = jnp.asarray(np.kron(np.eye(nkv, dtype=np.float32),
                                np.ones((head_dim, head_dim), np.float32)), BF16)
    attn_bias = _attn_bias(n_heads, nkv)

    h = x.reshape(m, d)
    xs = jnp.zeros((m * TOP_K + ne * tm, d // 2), U32)
    k = vt = None
    for layer in range(depth):
        sh_m, sc_m, g_m, sh_f, sc_f, g_f = jnp.split(mods[layer], 6, axis=-1)
        n = _norm_mod(h, norm_mix[layer], sc_m, sh_m, seq)
        if layer < n_a:
            y = _conv_in(n, w_in, a_conv, layer, seq)
            h = _matmul_res(y, w_out, h, g_m, layer, seq)
        else:
            if layer == n_a:
                sh_kv, sc_kv = jnp.split(kv_mods, 2, axis=-1)
                n_kv = _norm_mod(h, kv_norm, sc_kv, sh_kv, seq)
                k, vt = _kv_proj(n_kv, w_k, w_vt, jnp.tile(k_norm, nkv).reshape(1, kvd), bd_kv, head_dim, seq)
            i = layer - n_a
            q_gain = jnp.broadcast_to((jnp.tile(b_q_norm[i], group) * head_dim ** -0.5)[:, None], (gw, attn_bm))
            sink = jnp.repeat(b_sinks[i].reshape(nkv, group), WINDOW, axis=1).reshape(nkv, 1, group * WINDOW)
            a = _q_attn(n, w_qt, q_gain, k, vt, attn_bias, sink, i, head_dim, seq)
            h = _matmul_res(a, w_o, h, g_m, i, seq)
        h, xs = _moe_layer(h, norm_ffn[layer], sc_f, sh_f, g_f, rw, rb, w_gu, b_gu, w_d, b_d, xs, layer, seq, tm)
    return h.reshape(bsz, seq, d)
```

```python
import functools

import numpy as np
import jax
import jax.numpy as jnp
from jax import lax
from jax.experimental import pallas as pl
from jax.experimental.pallas import tpu as pltpu

WINDOW = 128
TOP_K = 4
CONV_W = 3
NORM_EPS = 1e-5
QK_EPS = 1e-6
SWIGLU_ALPHA = 1.702
SWIGLU_LIMIT = 7.0
LANES = 128
SUBLANES = 8
NEG_BIG = -1e30
MOE_OUT_CHUNKS = 4
MIB = 1024 * 1024

F32 = jnp.float32
BF16 = jnp.bfloat16
U32 = jnp.uint32
I32 = jnp.int32
HI_MASK = np.uint32(0xFFFF0000)


def _tile(total, pref):
    t = min(total, pref)
    assert total % t == 0, (total, pref)
    return t


def _params(sem, vmem_mib):
    return pltpu.CompilerParams(dimension_semantics=sem, vmem_limit_bytes=vmem_mib * MIB)


def _pack_bf16_pair(a, b):
    lo = lax.bitcast_convert_type(a.astype(BF16).astype(F32), U32) >> 16
    hi = lax.bitcast_convert_type(b.astype(BF16).astype(F32), U32) & HI_MASK
    return lo | hi


def _unpack_bf16_pair(w):
    lo = lax.bitcast_convert_type(w << 16, F32)
    hi = lax.bitcast_convert_type(w & HI_MASK, F32)
    return lo, hi


def _adaln_kernel(c_ref, wd_ref, wu_ref, b_ref, o_ref, d_ref):
    @pl.when(pl.program_id(1) == 0)
    def _():
        c = c_ref[...]
        s = c * jax.nn.sigmoid(c)
        d_ref[...] = jnp.dot(s, wd_ref[0], preferred_element_type=F32, precision=lax.Precision.HIGHEST)

    o_ref[0] = jnp.dot(d_ref[...], wu_ref[0], preferred_element_type=F32,
                       precision=lax.Precision.HIGHEST) + b_ref[0]


def _adaln(c, w_down, w_up, b):
    nl, d, r = w_down.shape
    n = w_up.shape[2]
    bsz = c.shape[0]
    tn = _tile(n, min(d, 2048))
    return pl.pallas_call(
        _adaln_kernel,
        grid=(nl, n // tn),
        in_specs=[
            pl.BlockSpec((bsz, d), lambda l, j: (0, 0)),
            pl.BlockSpec((1, d, r), lambda l, j: (l, 0, 0)),
            pl.BlockSpec((1, r, tn), lambda l, j: (l, 0, j)),
            pl.BlockSpec((1, 1, tn), lambda l, j: (l, 0, j)),
        ],
        out_specs=pl.BlockSpec((1, bsz, tn), lambda l, j: (l, 0, j)),
        out_shape=jax.ShapeDtypeStruct((nl, bsz, n), F32),
        scratch_shapes=[pltpu.VMEM((bsz, r), F32)],
        compiler_params=_params(("arbitrary", "arbitrary"), 32),
        name="adaln",
    )(c, w_down, w_up, b.reshape(nl, 1, n))


def _norm_mod_value(h_ref, g_ref, sc_ref, sh_ref):
    x = h_ref[...]
    ms = jnp.mean(x * x, axis=-1, keepdims=True)
    y = x * lax.rsqrt(ms + NORM_EPS) * g_ref[...]
    return y * (1.0 + sc_ref[0]) + sh_ref[0]


def _norm_kernel(h_ref, g_ref, sc_ref, sh_ref, o_ref):
    o_ref[...] = _norm_mod_value(h_ref, g_ref, sc_ref, sh_ref).astype(BF16)


def _norm_mod(h, gain, scale, shift, seq):
    m, d = h.shape
    bsz = scale.shape[0]
    bm = _tile(seq, 256)
    row_b = lambda i: ((i * bm) // seq, 0, 0)
    return pl.pallas_call(
        _norm_kernel,
        grid=(m // bm,),
        in_specs=[
            pl.BlockSpec((bm, d), lambda i: (i, 0)),
            pl.BlockSpec((1, d), lambda i: (0, 0)),
            pl.BlockSpec((1, 1, d), row_b),
            pl.BlockSpec((1, 1, d), row_b),
        ],
        out_specs=pl.BlockSpec((bm, d), lambda i: (i, 0)),
        out_shape=jax.ShapeDtypeStruct((m, d), BF16),
        compiler_params=_params(("arbitrary",), 40),
        name="norm_mod",
    )(h, gain.reshape(1, d), scale.reshape(bsz, 1, d), shift.reshape(bsz, 1, d))


def _norm_router_kernel(h_ref, g_ref, sc_ref, sh_ref, rw_ref, rb_ref,
                        np_ref, ri_ref, wt_ref, cnt_ref, carry_ref):
    @pl.when(pl.program_id(0) == 0)
    def _():
        carry_ref[...] = jnp.zeros_like(carry_ref)

    y = _norm_mod_value(h_ref, g_ref, sc_ref, sh_ref)
    half = y.shape[1] // 2
    np_ref[...] = _pack_bf16_pair(y[:, :half], y[:, half:])
    logits = jnp.dot(y.astype(BF16), rw_ref[...], preferred_element_type=F32) + rb_ref[...]
    bm = logits.shape[0]
    lane = lax.broadcasted_iota(I32, logits.shape, 1)

    l = logits
    sels, vals, idxs = [], [], []
    for _ in range(TOP_K):
        mx = jnp.max(l, axis=-1, keepdims=True)
        idx = jnp.min(jnp.where(l == mx, lane, LANES), axis=-1, keepdims=True)
        sel = lane == idx
        l = jnp.where(sel, -jnp.inf, l)
        sels.append(sel)
        vals.append(mx)
        idxs.append(idx)
    exps = [jnp.exp(v - vals[0]) for v in vals]
    denom = exps[0]
    for e in exps[1:]:
        denom = denom + e

    mask = sels[0].astype(F32)
    for sel in sels[1:]:
        mask = mask + sel.astype(F32)
    rr = lax.broadcasted_iota(I32, (bm, bm), 0)
    cc = lax.broadcasted_iota(I32, (bm, bm), 1)
    tri = (cc < rr).astype(BF16)
    carry = carry_ref[0:1, :]
    prefix = jnp.dot(tri, mask.astype(BF16), preferred_element_type=F32) + carry

    ri = jnp.zeros(logits.shape, I32)
    wt = jnp.zeros(logits.shape, F32)
    for k in range(TOP_K):
        rank = jnp.sum(jnp.where(sels[k], prefix, 0.0), axis=-1, keepdims=True).astype(I32)
        ri = jnp.where(lane == k, idxs[k], ri)
        ri = jnp.where(lane == TOP_K + k, rank, ri)
        wt = jnp.where(lane == k, exps[k] / denom, wt)
    ri_ref[...] = ri
    wt_ref[...] = wt
    new_carry = jnp.broadcast_to(carry + jnp.sum(mask, axis=0, keepdims=True), carry_ref.shape)
    carry_ref[...] = new_carry
    cnt_ref[...] = new_carry


def _norm_router(h, gain, scale, shift, rw, rb, layer, seq):
    m, d = h.shape
    bsz = scale.shape[0]
    bm = _tile(seq, 256)
    row_b = lambda i: ((i * bm) // seq, 0, 0)
    return pl.pallas_call(
        _norm_router_kernel,
        grid=(m // bm,),
        in_specs=[
            pl.BlockSpec((bm, d), lambda i: (i, 0)),
            pl.BlockSpec((1, d), lambda i: (0, 0)),
            pl.BlockSpec((1, 1, d), row_b),
            pl.BlockSpec((1, 1, d), row_b),
            pl.BlockSpec((None, d, LANES), lambda i: (layer, 0, 0)),
            pl.BlockSpec((None, 1, LANES), lambda i: (layer, 0, 0)),
        ],
        out_specs=[
            pl.BlockSpec((bm, d // 2), lambda i: (i, 0)),
            pl.BlockSpec((bm, LANES), lambda i: (i, 0)),
            pl.BlockSpec((bm, LANES), lambda i: (i, 0)),
            pl.BlockSpec((SUBLANES, LANES), lambda i: (0, 0)),
        ],
        out_shape=[
            jax.ShapeDtypeStruct((m, d // 2), U32),
            jax.ShapeDtypeStruct((m, LANES), I32),
            jax.ShapeDtypeStruct((m, LANES), F32),
            jax.ShapeDtypeStruct((SUBLANES, LANES), F32),
        ],
        scratch_shapes=[pltpu.VMEM((SUBLANES, LANES), F32)],
        compiler_params=_params(("arbitrary",), 40),
        name="norm_router",
    )(h, gain.reshape(1, d), scale.reshape(bsz, 1, d), shift.reshape(bsz, 1, d), rw, rb)


def _conv_in_kernel(n_ref, wb_ref, wc_ref, wv_ref, cw_ref, y_ref, tail_ref, *, seq):
    i = pl.program_id(0)
    j = pl.program_id(1)
    x = n_ref[...]
    bm = x.shape[0]
    b_gate = jnp.dot(x, wb_ref[...], preferred_element_type=F32)
    c_gate = jnp.dot(x, wc_ref[...], preferred_element_type=F32)
    v = jnp.dot(x, wv_ref[...], preferred_element_type=F32)
    u = c_gate * v
    prev = jnp.where((i * bm) % seq == 0, 0.0, tail_ref[j])
    tail_ref[j] = u[bm - SUBLANES:, :]
    u1 = pltpu.roll(u, 1, axis=0)
    u2 = pltpu.roll(u, 2, axis=0)
    row = lax.broadcasted_iota(I32, prev.shape, 0)
    head1 = jnp.where(row < 1, pltpu.roll(prev, 1, axis=0), u1[:SUBLANES])
    head2 = jnp.where(row < 2, pltpu.roll(prev, 2, axis=0), u2[:SUBLANES])
    u1 = jnp.concatenate([head1, u1[SUBLANES:]], axis=0)
    u2 = jnp.concatenate([head2, u2[SUBLANES:]], axis=0)
    cw = cw_ref[...]
    conv = cw[0:1, :] * u2 + cw[1:2, :] * u1 + cw[2:3, :] * u
    y_ref[...] = (b_gate * conv).astype(BF16)


def _conv_in(n, w_in, conv_w, layer, seq):
    m, d = n.shape
    bm = _tile(seq, 1024)
    bn = _tile(d, 256)
    nj = d // bn
    return pl.pallas_call(
        functools.partial(_conv_in_kernel, seq=seq),
        grid=(m // bm, nj),
        in_specs=[
            pl.BlockSpec((bm, d), lambda i, j: (i, 0)),
            pl.BlockSpec((None, d, bn), lambda i, j: (layer, 0, j)),
            pl.BlockSpec((None, d, bn), lambda i, j: (layer, 0, nj + j)),
            pl.BlockSpec((None, d, bn), lambda i, j: (layer, 0, 2 * nj + j)),
            pl.BlockSpec((None, CONV_W, bn), lambda i, j: (layer, 0, j)),
        ],
        out_specs=pl.BlockSpec((bm, bn), lambda i, j: (i, j)),
        out_shape=jax.ShapeDtypeStruct((m, d), BF16),
        scratch_shapes=[pltpu.VMEM((nj, SUBLANES, bn), F32)],
        compiler_params=_params(("arbitrary", "arbitrary"), 48),
        name="conv_in",
    )(n, w_in, w_in, w_in, conv_w)


def _matmul_res_kernel(a_ref, w_ref, h_ref, g_ref, o_ref):
    acc = jnp.dot(a_ref[...], w_ref[...], preferred_element_type=F32)
    o_ref[...] = h_ref[...] + g_ref[0] * acc


def _matmul_res(a, w, h, gate, layer, seq):
    m, k = a.shape
    n = w.shape[2]
    bsz = gate.shape[0]
    bm = _tile(seq, 1024)
    bn = _tile(n, 512)
    return pl.pallas_call(
        _matmul_res_kernel,
        grid=(m // bm, n // bn),
        in_specs=[
            pl.BlockSpec((bm, k), lambda i, j: (i, 0)),
            pl.BlockSpec((None, k, bn), lambda i, j: (layer, 0, j)),
            pl.BlockSpec((bm, bn), lambda i, j: (i, j)),
            pl.BlockSpec((1, 1, bn), lambda i, j: ((i * bm) // seq, 0, j)),
        ],
        out_specs=pl.BlockSpec((bm, bn), lambda i, j: (i, j)),
        out_shape=jax.ShapeDtypeStruct((m, n), F32),
        compiler_params=_params(("arbitrary", "arbitrary"), 48),
        name="matmul_res",
    )(a, w, h, gate.reshape(bsz, 1, n))


def _group_sumsq(t, bd_ref):
    tt = t * t
    hi = tt.astype(BF16)
    lo = (tt - hi.astype(F32)).astype(BF16)
    bd = bd_ref[...]
    return jnp.dot(hi, bd, preferred_element_type=F32) + jnp.dot(lo, bd, preferred_element_type=F32)


def _kv_kernel(n_ref, wk_ref, wvt_ref, gain_ref, bd_ref, k_ref, vt_ref, *, head_dim):
    x = n_ref[...]
    k = jnp.dot(x, wk_ref[...], preferred_element_type=F32)
    ssq = _group_sumsq(k, bd_ref)
    kn = k * lax.rsqrt(ssq * (1.0 / head_dim) + QK_EPS) * gain_ref[...]
    for g in range(k.shape[1] // head_dim):
        k_ref[g] = kn[:, g * head_dim:(g + 1) * head_dim].astype(BF16)
    vt_ref[...] = lax.dot_general(wvt_ref[...], x, (((1,), (1,)), ((), ())),
                                  preferred_element_type=F32).astype(BF16)


def _kv_proj(n, w_k, w_vt, k_gain, bd, head_dim, seq):
    m, d = n.shape
    kvd = w_k.shape[1]
    nkv = kvd // head_dim
    bm = _tile(seq, 512)
    return pl.pallas_call(
        functools.partial(_kv_kernel, head_dim=head_dim),
        grid=(m // bm,),
        in_specs=[
            pl.BlockSpec((bm, d), lambda i: (i, 0)),
            pl.BlockSpec((d, kvd), lambda i: (0, 0)),
            pl.BlockSpec((kvd, d), lambda i: (0, 0)),
            pl.BlockSpec((1, kvd), lambda i: (0, 0)),
            pl.BlockSpec((kvd, kvd), lambda i: (0, 0)),
        ],
        out_specs=[pl.BlockSpec((nkv, bm, head_dim), lambda i: (0, i, 0)),
                   pl.BlockSpec((kvd, bm), lambda i: (0, i))],
        out_shape=[jax.ShapeDtypeStruct((nkv, m, head_dim), BF16),
                   jax.ShapeDtypeStruct((kvd, m), BF16)],
        compiler_params=_params(("arbitrary",), 48),
        name="kv_proj",
    )(n, w_k, w_vt, k_gain, bd)


def _q_attn_kernel(n_ref, wqt_ref, gain_ref, kc_ref, kp_ref, vc_ref, vp_ref, bias_ref, sink_ref,
                   o_ref, *, seq, head_dim):
    i = pl.program_id(0)
    bm = n_ref.shape[0]
    group = wqt_ref.shape[0] // head_dim
    qt = lax.dot_general(wqt_ref[...], n_ref[...], (((1,), (1,)), ((), ())), preferred_element_type=F32)
    gain = gain_ref[...]
    qn = []
    for hh in range(group):
        qh = qt[hh * head_dim:(hh + 1) * head_dim, :]
        rs = lax.rsqrt(jnp.sum(qh * qh, axis=0, keepdims=True) * (1.0 / head_dim) + QK_EPS)
        qn.append((qh * rs * gain[hh * head_dim:(hh + 1) * head_dim, :]).astype(BF16))
    sink = sink_ref[...]
    shape = (WINDOW, group * WINDOW)
    key_c = lax.broadcasted_iota(I32, shape, 0)
    query_i = lax.broadcasted_iota(I32, shape, 1) & (WINDOW - 1)
    in_cur = key_c <= query_i
    for qb in range(bm // WINDOW):
        r0 = qb * WINDOW
        q_all = jnp.concatenate([q[:, r0:r0 + WINDOW] for q in qn], axis=1)
        if qb == 0:
            k_prev, vt_prev = kp_ref[0], vp_ref[...]
        else:
            k_prev, vt_prev = kc_ref[0, r0 - WINDOW:r0, :], vc_ref[:, r0 - WINDOW:r0]
        k2 = jnp.concatenate([k_prev, kc_ref[0, r0:r0 + WINDOW, :]], axis=0)
        v2t = jnp.concatenate([vt_prev, vc_ref[:, r0:r0 + WINDOW]], axis=1)
        st = jnp.dot(k2, q_all, preferred_element_type=F32)
        first = ((i * bm + r0) % seq == 0).astype(I32)
        s = jnp.where(in_cur, st[WINDOW:], st[:WINDOW]) + bias_ref[first]
        mx = jnp.maximum(jnp.max(s, axis=0, keepdims=True), sink)
        p = jnp.exp(s - mx)
        denom = jnp.sum(p, axis=0, keepdims=True) + jnp.exp(sink - mx)
        pt = jnp.concatenate([jnp.where(in_cur, 0.0, p), jnp.where(in_cur, p, 0.0)], axis=0).astype(BF16)
        ot = jnp.dot(v2t, pt, preferred_element_type=F32) * (1.0 / denom)
        for pr in range(group // 2):
            blk = jnp.concatenate([ot[:, (2 * pr) * WINDOW:(2 * pr + 1) * WINDOW],
                                   ot[:, (2 * pr + 1) * WINDOW:(2 * pr + 2) * WINDOW]], axis=0)
            o_ref[r0:r0 + WINDOW, pr * 2 * head_dim:(pr + 1) * 2 * head_dim] = blk.T.astype(BF16)


def _q_attn(n, w_qt, q_gain, k, vt, bias, sink, layer, head_dim, seq):
    m, d = n.shape
    nkv = k.shape[0]
    gw = w_qt.shape[1] // nkv
    group = gw // head_dim
    assert head_dim == WINDOW // 2 and group % 2 == 0
    bm = q_gain.shape[1]
    nb = bm // WINDOW
    return pl.pallas_call(
        functools.partial(_q_attn_kernel, seq=seq, head_dim=head_dim),
        grid=(m // bm, nkv),
        in_specs=[
            pl.BlockSpec((bm, d), lambda i, g: (i, 0)),
            pl.BlockSpec((None, gw, d), lambda i, g: (layer, g, 0)),
            pl.BlockSpec((gw, bm), lambda i, g: (0, 0)),
            pl.BlockSpec((1, bm, head_dim), lambda i, g: (g, i, 0)),
            pl.BlockSpec((1, WINDOW, head_dim), lambda i, g: (g, jnp.maximum(i * nb - 1, 0), 0)),
            pl.BlockSpec((head_dim, bm), lambda i, g: (g, i)),
            pl.BlockSpec((head_dim, WINDOW), lambda i, g: (g, jnp.maximum(i * nb - 1, 0))),
            pl.BlockSpec((None, 2, WINDOW, group * WINDOW), lambda i, g: (g, 0, 0, 0)),
            pl.BlockSpec((None, 1, group * WINDOW), lambda i, g: (g, 0, 0)),
        ],
        out_specs=pl.BlockSpec((bm, gw), lambda i, g: (i, g)),
        out_shape=jax.ShapeDtypeStruct((m, w_qt.shape[1]), BF16),
        compiler_params=_params(("arbitrary", "arbitrary"), 48),
        name="q_attn",
    )(n, w_qt, q_gain, k, k, vt, vt, bias, sink)


def _attn_bias(n_heads, nkv):
    group = n_heads // nkv
    h = np.arange(1, n_heads + 1, dtype=np.float32)
    slopes = np.power(2.0, -8.0 * h / n_heads).astype(np.float32)
    c = np.arange(WINDOW)[:, None]
    qi = np.arange(WINDOW)[None, :]
    in_cur = c <= qi
    dist = np.where(in_cur, qi - c, WINDOW + qi - c).astype(np.float32)
    alibi = -slopes[:, None, None] * dist[None]
    first = np.where(in_cur[None], alibi, -np.inf).astype(np.float32)
    tables = np.stack([alibi, first], axis=1)
    tables = tables.reshape(nkv, group, 2, WINDOW, WINDOW).transpose(0, 2, 3, 1, 4)
    return jnp.asarray(tables.reshape(nkv, 2, WINDOW, group * WINDOW))


def _expert_ffn(x_ref, y_ref, wgu_ref, bgu_ref, wd_ref, bd_ref):
    lo, hi = _unpack_bf16_pair(x_ref[...])
    half = lo.shape[1]
    f = wd_ref.shape[1]
    x = jnp.concatenate([lo.astype(BF16), hi.astype(BF16)], axis=1)
    gu = jnp.dot(x, wgu_ref[0], preferred_element_type=F32) + bgu_ref[0]
    x_glu = jnp.minimum(gu[:, :f], SWIGLU_LIMIT)
    x_lin = jnp.clip(gu[:, f:], -SWIGLU_LIMIT, SWIGLU_LIMIT)
    hid = (x_glu * jax.nn.sigmoid(SWIGLU_ALPHA * x_glu) * (x_lin + 1.0)).astype(BF16)
    cw = half // MOE_OUT_CHUNKS
    for ch in range(MOE_OUT_CHUNKS):
        a0, b0 = ch * cw, half + ch * cw
        ya = jnp.dot(hid, wd_ref[0, :, a0:a0 + cw], preferred_element_type=F32) + bd_ref[0, :, a0:a0 + cw]
        yb = jnp.dot(hid, wd_ref[0, :, b0:b0 + cw], preferred_element_type=F32) + bd_ref[0, :, b0:b0 + cw]
        y_ref[:, a0:a0 + cw] = _pack_bf16_pair(ya, yb)


def _moe_kernel(te_ref, nv_ref, src_cur_ref, src_nxt_ref, dst_cur_ref,
                n_hbm, wgu_ref, bgu_ref, wd_ref, bd_ref, y_hbm,
                xbuf0, xbuf1, ybuf0, ybuf1, gsem, ssem, *, tm):
    del te_ref
    j = pl.program_id(0)
    last = pl.num_programs(0) - 1
    valid = j < nv_ref[0]

    def gather_row(src_ref, i, xb, sem):
        pltpu.make_async_copy(n_hbm.at[pl.ds(src_ref[0, 0, i], 1)], xb.at[pl.ds(i, 1)], sem).start()

    def gather_wait(xb, sem):
        pltpu.make_async_copy(n_hbm.at[pl.ds(0, tm)], xb, sem).wait()

    def scatter_wait(yb, sem):
        pltpu.make_async_copy(yb, y_hbm.at[pl.ds(0, tm)], sem).wait()

    @pl.when(j == 0)
    def _():
        for i in range(tm):
            gather_row(src_cur_ref, i, xbuf0, gsem.at[0])

    def step(par):
        xb_cur, xb_nxt = (xbuf0, xbuf1) if par == 0 else (xbuf1, xbuf0)
        yb_cur, yb_prev = (ybuf0, ybuf1) if par == 0 else (ybuf1, ybuf0)
        g_cur, g_nxt = gsem.at[par], gsem.at[1 - par]
        s_cur, s_prev = ssem.at[par], ssem.at[1 - par]

        @pl.when(j < last)
        def _():
            for i in range(tm):
                gather_row(src_nxt_ref, i, xb_nxt, g_nxt)

        gather_wait(xb_cur, g_cur)

        @pl.when(j >= 2)
        def _():
            scatter_wait(yb_cur, s_cur)

        @pl.when(valid)
        def _():
            _expert_ffn(xb_cur, yb_cur, wgu_ref, bgu_ref, wd_ref, bd_ref)

        @pl.when(jnp.logical_not(valid))
        def _():
            yb_cur[...] = jnp.zeros_like(yb_cur)

        for i in range(tm):
            pltpu.make_async_copy(yb_cur.at[pl.ds(i, 1)], y_hbm.at[pl.ds(dst_cur_ref[0, 0, i], 1)],
                                  s_cur).start()

        @pl.when(j == last)
        def _():
            scatter_wait(yb_prev, s_prev)
            scatter_wait(yb_cur, s_cur)

    @pl.when(j % 2 == 0)
    def _():
        step(0)

    @pl.when(j % 2 == 1)
    def _():
        step(1)


def _moe_grouped(tile_expert, n_valid, src, dst, n_packed, w_gu, b_gu, w_d, b_d, layer, tm):
    n_tiles = src.shape[0]
    dh = n_packed.shape[1]
    _, ne, d, f2 = w_gu.shape
    f = f2 // 2
    assert dh % MOE_OUT_CHUNKS == 0 and n_tiles >= 2
    by_expert = lambda j, te, nv: (layer, te[j], 0, 0)
    smem_tile = lambda shift: pl.BlockSpec(
        (1, 1, tm), lambda j, te, nv: (jnp.minimum(j + shift, n_tiles - 1), 0, 0), memory_space=pltpu.SMEM)
    grid_spec = pltpu.PrefetchScalarGridSpec(
        num_scalar_prefetch=2,
        grid=(n_tiles,),
        in_specs=[
            smem_tile(0),
            smem_tile(1),
            smem_tile(0),
            pl.BlockSpec(memory_space=pl.ANY),
            pl.BlockSpec((None, 1, d, f2), by_expert),
            pl.BlockSpec((None, 1, 1, f2), by_expert),
            pl.BlockSpec((None, 1, f, d), by_expert),
            pl.BlockSpec((None, 1, 1, d), by_expert),
        ],
        out_specs=pl.BlockSpec(memory_space=pl.ANY),
        scratch_shapes=[pltpu.VMEM((tm, dh), U32)] * 4 + [pltpu.SemaphoreType.DMA((2,))] * 2,
    )
    return pl.pallas_call(
        functools.partial(_moe_kernel, tm=tm),
        grid_spec=grid_spec,
        out_shape=jax.ShapeDtypeStruct((n_tiles * tm, dh), U32),
        compiler_params=_params(("arbitrary",), 56),
        name="moe_grouped",
    )(tile_expert, n_valid, src, src, dst, n_packed, w_gu, b_gu, w_d, b_d)


COMBINE_ROWS = 16
COMBINE_WORDS = 256


def _combine_kernel(*refs, n_norms):
    y_ref, h_ref, wt_ref, g_ref = refs[:4]
    norm_refs = refs[4:4 + 2 * n_norms]
    o_ref = refs[4 + 2 * n_norms]
    n_refs = refs[5 + 2 * n_norms:]
    bc, d = h_ref.shape
    dh = d // 2
    rows, cw = COMBINE_ROWS, COMBINE_WORDS

    def body(r, carry):
        rs = pl.ds(pl.multiple_of(r * rows, rows), rows)
        wt = wt_ref[rs, :]
        wk = [wt[:, k:k + 1] for k in range(TOP_K)]
        ssq = jnp.zeros((rows, LANES), F32)
        for c in range(dh // cw):
            acc_lo = acc_hi = None
            for k in range(TOP_K):
                lo, hi = _unpack_bf16_pair(y_ref[rs, k * dh + c * cw:k * dh + (c + 1) * cw])
                acc_lo = wk[k] * lo if acc_lo is None else acc_lo + wk[k] * lo
                acc_hi = wk[k] * hi if acc_hi is None else acc_hi + wk[k] * hi
            a0, b0 = c * cw, dh + c * cw
            ha = h_ref[rs, a0:a0 + cw] + g_ref[0, :, a0:a0 + cw] * acc_lo
            hb = h_ref[rs, b0:b0 + cw] + g_ref[0, :, b0:b0 + cw] * acc_hi
            o_ref[rs, a0:a0 + cw] = ha
            o_ref[rs, b0:b0 + cw] = hb
            if n_norms:
                for t in range(cw // LANES):
                    va = ha[:, t * LANES:(t + 1) * LANES]
                    vb = hb[:, t * LANES:(t + 1) * LANES]
                    ssq = ssq + va * va + vb * vb
        if n_norms:
            rsq = lax.rsqrt(jnp.sum(ssq, axis=-1, keepdims=True) * (1.0 / d) + NORM_EPS)
            for c in range(d // (2 * cw)):
                cs = slice(c * 2 * cw, (c + 1) * 2 * cw)
                xn = o_ref[rs, cs] * rsq
                for t in range(n_norms):
                    gs_ref, sh_ref = norm_refs[2 * t], norm_refs[2 * t + 1]
                    n_refs[t][rs, cs] = (xn * gs_ref[0, :, cs] + sh_ref[0, :, cs]).astype(BF16)
        return carry

    lax.fori_loop(0, bc // rows, body, 0)


def _combine(y, h, wt, gate, norms, seq):
    m, d = h.shape
    bsz = gate.shape[0]
    bc = _tile(seq, 256)
    per_batch = pl.BlockSpec((1, 1, d), lambda i: ((i * bc) // seq, 0, 0))
    row_tile = pl.BlockSpec((bc, d), lambda i: (i, 0))
    norm_args = [v.reshape(bsz, 1, d) for pair in norms for v in pair]
    outs = pl.pallas_call(
        functools.partial(_combine_kernel, n_norms=len(norms)),
        grid=(m // bc,),
        in_specs=[
            pl.BlockSpec((bc, y.shape[1]), lambda i: (i, 0)),
            row_tile,
            pl.BlockSpec((bc, LANES), lambda i: (i, 0)),
            per_batch,
        ] + [per_batch] * len(norm_args),
        out_specs=[row_tile] * (1 + len(norms)),
        out_shape=[jax.ShapeDtypeStruct((m, d), F32)] + [jax.ShapeDtypeStruct((m, d), BF16)] * len(norms),
        compiler_params=_params(("arbitrary",), 48),
        name="moe_combine",
    )(y, h, wt, gate.reshape(bsz, 1, d), *norm_args)
    return outs[0], list(outs[1:])


def _moe_layer(h, gain, scale, shift, gate, rw, rb, w_gu, b_gu, w_d, b_d, next_norms, layer, seq, tm):
    m, d = h.shape
    ne = w_gu.shape[1]
    n_packed, route, wt, counts = _norm_router(h, gain, scale, shift, rw, rb, layer, seq)

    cnt = counts[0, :ne].astype(I32)
    padded = ((cnt + tm - 1) // tm) * tm
    ends = jnp.cumsum(padded)
    starts = ends - padded
    expert_ids = jnp.arange(ne, dtype=I32)
    idx = route[:, :TOP_K]
    pos = jnp.sum(jnp.where(idx[:, :, None] == expert_ids, starts, 0), axis=-1) + route[:, TOP_K:2 * TOP_K]
    rows = m * TOP_K + ne * tm
    n_tiles = rows // tm
    n_valid = ends[-1] // tm
    tile_id = jnp.arange(n_tiles, dtype=I32)
    tile_start = jnp.minimum(tile_id, n_valid - 1) * tm
    tile_expert = jnp.sum((ends[None, :] <= tile_start[:, None]).astype(I32), axis=1)

    slot_of_row = jnp.full((rows,), -1, I32).at[pos.reshape(-1)].set(
        jnp.arange(m * TOP_K, dtype=I32), unique_indices=True)
    is_pad = slot_of_row < 0
    pad_rank = jnp.cumsum(is_pad.astype(I32)) - 1
    src = jnp.where(is_pad, 0, slot_of_row // TOP_K)
    dst = jnp.where(is_pad, m * TOP_K + pad_rank, slot_of_row)
    src = src.reshape(n_tiles, 1, tm)
    dst = dst.reshape(n_tiles, 1, tm)

    y = _moe_grouped(tile_expert, n_valid.reshape(1).astype(I32), src, dst, n_packed, w_gu, b_gu, w_d, b_d,
                     layer, tm)
    return _combine(y.reshape(-1, TOP_K * (d // 2)), h, wt, gate, next_norms, seq)


def kernel(x, c, norm_mix, norm_ffn, mod_down, mod_up, mod_b, a_w_in, a_conv, a_w_out, kv_norm, kv_mod_down, kv_mod_up, kv_mod_b, w_kv, k_norm, b_w_q, b_q_norm, b_sinks, b_w_o, router_w, router_b, moe_w_gu, moe_b_gu, moe_w_down, moe_b_down):
    bsz, seq, d = x.shape
    m = bsz * seq
    depth = norm_mix.shape[0]
    n_a = a_w_in.shape[0]
    head_dim = k_norm.shape[0]
    n_heads = b_sinks.shape[1]
    nkv = w_kv.shape[1] // (2 * head_dim)
    group = n_heads // nkv
    ne = router_w.shape[2]
    tm = _tile(m * TOP_K, 512)
    attn_bm = _tile(seq, 512)
    assert seq % WINDOW == 0 and n_a >= 1

    mods = _adaln(c, mod_down, mod_up, mod_b)
    kv_mods = _adaln(c, kv_mod_down[None], kv_mod_up[None], kv_mod_b[None])[0]

    w_in = a_w_in.astype(BF16)
    w_out = a_w_out.astype(BF16)
    w_qt = jnp.swapaxes(b_w_q, 1, 2).astype(BF16)
    w_o = b_w_o.astype(BF16)
    kvd = nkv * head_dim
    w_k = w_kv[:, :kvd].astype(BF16)
    w_vt = w_kv[:, kvd:].T.astype(BF16)
    w_gu = moe_w_gu.astype(BF16)
    w_d = moe_w_down.astype(BF16)
    b_gu = moe_b_gu[:, :, None, :]
    b_d = moe_b_down[:, :, None, :]
    rw = jnp.zeros((depth, d, LANES), BF16).at[:, :, :ne].set(router_w.astype(BF16))
    rb = jnp.full((depth, 1, LANES), NEG_BIG, F32).at[:, 0, :ne].set(router_b)

    gw = group * head_dim
    bd_kv = jnp.asarray(np.kron(np.eye(nkv, dtype=np.float32),
                                np.ones((head_dim, head_dim), np.float32)), BF16)
    attn_bias = _attn_bias(n_heads, nkv)

    h = x.reshape(m, d)
    sh_kv, sc_kv = jnp.split(kv_mods, 2, axis=-1)
    k = vt = None
    sh_m, sc_m = mods[0, :, :d], mods[0, :, d:2 * d]
    n = _norm_mod(h, norm_mix[0], sc_m, sh_m, seq)
    n_kv = None
    for layer in range(depth):
        _, _, g_m, sh_f, sc_f, g_f = jnp.split(mods[layer], 6, axis=-1)
        if layer < n_a:
            y = _conv_in(n, w_in, a_conv, layer, seq)
            h = _matmul_res(y, w_out, h, g_m, layer, seq)
        else:
            if layer == n_a:
                k, vt = _kv_proj(n_kv, w_k, w_vt, jnp.tile(k_norm, nkv).reshape(1, kvd), bd_kv, head_dim, seq)
            i = layer - n_a
            q_gain = jnp.broadcast_to((jnp.tile(b_q_norm[i], group) * head_dim ** -0.5)[:, None], (gw, attn_bm))
            sink = jnp.repeat(b_sinks[i].reshape(nkv, group), WINDOW, axis=1).reshape(nkv, 1, group * WINDOW)
            a = _q_attn(n, w_qt, q_gain, k, vt, attn_bias, sink, i, head_dim, seq)
            h = _matmul_res(a, w_o, h, g_m, i, seq)
        next_norms = []
        if layer + 1 < depth:
            sh_n, sc_n = mods[layer + 1, :, :d], mods[layer + 1, :, d:2 * d]
            next_norms.append((norm_mix[layer + 1] * (1.0 + sc_n), sh_n))
            if layer + 1 == n_a:
                next_norms.append((kv_norm * (1.0 + sc_kv), sh_kv))
        h, normed = _moe_layer(h, norm_ffn[layer], sc_f, sh_f, g_f, rw, rb, w_gu, b_gu, w_d, b_d,
                               next_norms, layer, seq, tm)
        if normed:
            n = normed[0]
            n_kv = normed[1] if len(normed) > 1 else None
    return h.reshape(bsz, seq, d)
```

```python
import functools

import numpy as np
import jax
import jax.numpy as jnp
from jax import lax
from jax.experimental import pallas as pl
from jax.experimental.pallas import tpu as pltpu

WINDOW = 128
TOP_K = 4
CONV_W = 3
NORM_EPS = 1e-5
QK_EPS = 1e-6
SWIGLU_ALPHA = 1.702
SWIGLU_LIMIT = 7.0
LANES = 128
SUBLANES = 8
NEG_BIG = -1e30
MOE_OUT_CHUNKS = 4
MOE_IN_CHUNKS = 8
MIB = 1024 * 1024

F32 = jnp.float32
BF16 = jnp.bfloat16
U32 = jnp.uint32
I32 = jnp.int32
HI_MASK = np.uint32(0xFFFF0000)


def _tile(total, pref):
    t = min(total, pref)
    assert total % t == 0, (total, pref)
    return t


def _params(sem, vmem_mib):
    return pltpu.CompilerParams(dimension_semantics=sem, vmem_limit_bytes=vmem_mib * MIB)


def _pack_bf16_pair(a, b):
    lo = lax.bitcast_convert_type(a.astype(BF16).astype(F32), U32) >> 16
    hi = lax.bitcast_convert_type(b.astype(BF16).astype(F32), U32) & HI_MASK
    return lo | hi


def _unpack_bf16_pair(w):
    lo = lax.bitcast_convert_type(w << 16, F32)
    hi = lax.bitcast_convert_type(w & HI_MASK, F32)
    return lo, hi


def _adaln_kernel(c_ref, wd_ref, wu_ref, b_ref, o_ref, d_ref):
    @pl.when(pl.program_id(1) == 0)
    def _():
        c = c_ref[...]
        s = c * jax.nn.sigmoid(c)
        d_ref[...] = jnp.dot(s, wd_ref[0], preferred_element_type=F32, precision=lax.Precision.HIGHEST)

    o_ref[0] = jnp.dot(d_ref[...], wu_ref[0], preferred_element_type=F32,
                       precision=lax.Precision.HIGHEST) + b_ref[0]


def _adaln(c, w_down, w_up, b):
    nl, d, r = w_down.shape
    n = w_up.shape[2]
    bsz = c.shape[0]
    tn = _tile(n, min(d, 2048))
    return pl.pallas_call(
        _adaln_kernel,
        grid=(nl, n // tn),
        in_specs=[
            pl.BlockSpec((bsz, d), lambda l, j: (0, 0)),
            pl.BlockSpec((1, d, r), lambda l, j: (l, 0, 0)),
            pl.BlockSpec((1, r, tn), lambda l, j: (l, 0, j)),
            pl.BlockSpec((1, 1, tn), lambda l, j: (l, 0, j)),
        ],
        out_specs=pl.BlockSpec((1, bsz, tn), lambda l, j: (l, 0, j)),
        out_shape=jax.ShapeDtypeStruct((nl, bsz, n), F32),
        scratch_shapes=[pltpu.VMEM((bsz, r), F32)],
        compiler_params=_params(("arbitrary", "arbitrary"), 32),
        name="adaln",
    )(c, w_down, w_up, b.reshape(nl, 1, n))


def _norm_mod_value(h_ref, g_ref, sc_ref, sh_ref):
    x = h_ref[...]
    ms = jnp.mean(x * x, axis=-1, keepdims=True)
    y = x * lax.rsqrt(ms + NORM_EPS) * g_ref[...]
    return y * (1.0 + sc_ref[0]) + sh_ref[0]


def _norm_kernel(h_ref, g_ref, sc_ref, sh_ref, o_ref):
    o_ref[...] = _norm_mod_value(h_ref, g_ref, sc_ref, sh_ref).astype(BF16)


def _norm_mod(h, gain, scale, shift, seq):
    m, d = h.shape
    bsz = scale.shape[0]
    bm = _tile(seq, 256)
    row_b = lambda i: ((i * bm) // seq, 0, 0)
    return pl.pallas_call(
        _norm_kernel,
        grid=(m // bm,),
        in_specs=[
            pl.BlockSpec((bm, d), lambda i: (i, 0)),
            pl.BlockSpec((1, d), lambda i: (0, 0)),
            pl.BlockSpec((1, 1, d), row_b),
            pl.BlockSpec((1, 1, d), row_b),
        ],
        out_specs=pl.BlockSpec((bm, d), lambda i: (i, 0)),
        out_shape=jax.ShapeDtypeStruct((m, d), BF16),
        compiler_params=_params(("arbitrary",), 40),
        name="norm_mod",
    )(h, gain.reshape(1, d), scale.reshape(bsz, 1, d), shift.reshape(bsz, 1, d))


def _norm_router_kernel(h_ref, g_ref, sc_ref, sh_ref, rw_ref, rb_ref,
                        np_ref, ri_ref, wt_ref, cnt_ref, carry_ref):
    @pl.when(pl.program_id(0) == 0)
    def _():
        carry_ref[...] = jnp.zeros_like(carry_ref)

    y = _norm_mod_value(h_ref, g_ref, sc_ref, sh_ref)
    half = y.shape[1] // 2
    np_ref[...] = _pack_bf16_pair(y[:, :half], y[:, half:])
    logits = jnp.dot(y.astype(BF16), rw_ref[...], preferred_element_type=F32) + rb_ref[...]
    bm = logits.shape[0]
    lane = lax.broadcasted_iota(I32, logits.shape, 1)

    l = logits
    sels, vals, idxs = [], [], []
    for _ in range(TOP_K):
        mx = jnp.max(l, axis=-1, keepdims=True)
        idx = jnp.min(jnp.where(l == mx, lane, LANES), axis=-1, keepdims=True)
        sel = lane == idx
        l = jnp.where(sel, -jnp.inf, l)
        sels.append(sel)
        vals.append(mx)
        idxs.append(idx)
    exps = [jnp.exp(v - vals[0]) for v in vals]
    denom = exps[0]
    for e in exps[1:]:
        denom = denom + e

    mask = sels[0].astype(F32)
    for sel in sels[1:]:
        mask = mask + sel.astype(F32)
    rr = lax.broadcasted_iota(I32, (bm, bm), 0)
    cc = lax.broadcasted_iota(I32, (bm, bm), 1)
    tri = (cc < rr).astype(BF16)
    carry = carry_ref[0:1, :]
    prefix = jnp.dot(tri, mask.astype(BF16), preferred_element_type=F32) + carry

    ri = jnp.zeros(logits.shape, I32)
    wt = jnp.zeros(logits.shape, F32)
    for k in range(TOP_K):
        rank = jnp.sum(jnp.where(sels[k], prefix, 0.0), axis=-1, keepdims=True).astype(I32)
        ri = jnp.where(lane == k, idxs[k], ri)
        ri = jnp.where(lane == TOP_K + k, rank, ri)
        wt = jnp.where(lane == k, exps[k] / denom, wt)
    ri_ref[...] = ri
    wt_ref[...] = wt
    new_carry = jnp.broadcast_to(carry + jnp.sum(mask, axis=0, keepdims=True), carry_ref.shape)
    carry_ref[...] = new_carry
    cnt_ref[...] = new_carry


def _norm_router(h, gain, scale, shift, rw, rb, layer, seq):
    m, d = h.shape
    bsz = scale.shape[0]
    bm = _tile(seq, 256)
    row_b = lambda i: ((i * bm) // seq, 0, 0)
    return pl.pallas_call(
        _norm_router_kernel,
        grid=(m // bm,),
        in_specs=[
            pl.BlockSpec((bm, d), lambda i: (i, 0)),
            pl.BlockSpec((1, d), lambda i: (0, 0)),
            pl.BlockSpec((1, 1, d), row_b),
            pl.BlockSpec((1, 1, d), row_b),
            pl.BlockSpec((None, d, LANES), lambda i: (layer, 0, 0)),
            pl.BlockSpec((None, 1, LANES), lambda i: (layer, 0, 0)),
        ],
        out_specs=[
            pl.BlockSpec((bm, d // 2), lambda i: (i, 0)),
            pl.BlockSpec((bm, LANES), lambda i: (i, 0)),
            pl.BlockSpec((bm, LANES), lambda i: (i, 0)),
            pl.BlockSpec((SUBLANES, LANES), lambda i: (0, 0)),
        ],
        out_shape=[
            jax.ShapeDtypeStruct((m, d // 2), U32),
            jax.ShapeDtypeStruct((m, LANES), I32),
            jax.ShapeDtypeStruct((m, LANES), F32),
            jax.ShapeDtypeStruct((SUBLANES, LANES), F32),
        ],
        scratch_shapes=[pltpu.VMEM((SUBLANES, LANES), F32)],
        compiler_params=_params(("arbitrary",), 40),
        name="norm_router",
    )(h, gain.reshape(1, d), scale.reshape(bsz, 1, d), shift.reshape(bsz, 1, d), rw, rb)


def _conv_in_kernel(n_ref, wb_ref, wc_ref, wv_ref, cw_ref, y_ref, tail_ref, *, seq):
    i = pl.program_id(0)
    j = pl.program_id(1)
    x = n_ref[...]
    bm = x.shape[0]
    b_gate = jnp.dot(x, wb_ref[...], preferred_element_type=F32)
    c_gate = jnp.dot(x, wc_ref[...], preferred_element_type=F32)
    v = jnp.dot(x, wv_ref[...], preferred_element_type=F32)
    u = c_gate * v
    prev = jnp.where((i * bm) % seq == 0, 0.0, tail_ref[j])
    tail_ref[j] = u[bm - SUBLANES:, :]
    u1 = pltpu.roll(u, 1, axis=0)
    u2 = pltpu.roll(u, 2, axis=0)
    row = lax.broadcasted_iota(I32, prev.shape, 0)
    head1 = jnp.where(row < 1, pltpu.roll(prev, 1, axis=0), u1[:SUBLANES])
    head2 = jnp.where(row < 2, pltpu.roll(prev, 2, axis=0), u2[:SUBLANES])
    u1 = jnp.concatenate([head1, u1[SUBLANES:]], axis=0)
    u2 = jnp.concatenate([head2, u2[SUBLANES:]], axis=0)
    cw = cw_ref[...]
    conv = cw[0:1, :] * u2 + cw[1:2, :] * u1 + cw[2:3, :] * u
    y_ref[...] = (b_gate * conv).astype(BF16)


def _conv_in(n, w_in, conv_w, layer, seq):
    m, d = n.shape
    bm = _tile(seq, 1024)
    bn = _tile(d, 256)
    nj = d // bn
    return pl.pallas_call(
        functools.partial(_conv_in_kernel, seq=seq),
        grid=(m // bm, nj),
        in_specs=[
            pl.BlockSpec((bm, d), lambda i, j: (i, 0)),
            pl.BlockSpec((None, d, bn), lambda i, j: (layer, 0, j)),
            pl.BlockSpec((None, d, bn), lambda i, j: (layer, 0, nj + j)),
            pl.BlockSpec((None, d, bn), lambda i, j: (layer, 0, 2 * nj + j)),
            pl.BlockSpec((None, CONV_W, bn), lambda i, j: (layer, 0, j)),
        ],
        out_specs=pl.BlockSpec((bm, bn), lambda i, j: (i, j)),
        out_shape=jax.ShapeDtypeStruct((m, d), BF16),
        scratch_shapes=[pltpu.VMEM((nj, SUBLANES, bn), F32)],
        compiler_params=_params(("arbitrary", "arbitrary"), 48),
        name="conv_in",
    )(n, w_in, w_in, w_in, conv_w)


def _matmul_res_kernel(a_ref, w_ref, h_ref, g_ref, o_ref):
    acc = jnp.dot(a_ref[...], w_ref[...], preferred_element_type=F32)
    o_ref[...] = h_ref[...] + g_ref[0] * acc


def _matmul_res(a, w, h, gate, layer, seq):
    m, k = a.shape
    n = w.shape[2]
    bsz = gate.shape[0]
    bm = _tile(seq, 1024)
    bn = _tile(n, 512)
    return pl.pallas_call(
        _matmul_res_kernel,
        grid=(m // bm, n // bn),
        in_specs=[
            pl.BlockSpec((bm, k), lambda i, j: (i, 0)),
            pl.BlockSpec((None, k, bn), lambda i, j: (layer, 0, j)),
            pl.BlockSpec((bm, bn), lambda i, j: (i, j)),
            pl.BlockSpec((1, 1, bn), lambda i, j: ((i * bm) // seq, 0, j)),
        ],
        out_specs=pl.BlockSpec((bm, bn), lambda i, j: (i, j)),
        out_shape=jax.ShapeDtypeStruct((m, n), F32),
        compiler_params=_params(("arbitrary", "arbitrary"), 48),
        name="matmul_res",
    )(a, w, h, gate.reshape(bsz, 1, n))


def _group_sumsq(t, bd_ref):
    tt = t * t
    hi = tt.astype(BF16)
    lo = (tt - hi.astype(F32)).astype(BF16)
    bd = bd_ref[...]
    return jnp.dot(hi, bd, preferred_element_type=F32) + jnp.dot(lo, bd, preferred_element_type=F32)


def _kv_kernel(n_ref, wk_ref, wvt_ref, gain_ref, bd_ref, k_ref, vt_ref, *, head_dim):
    x = n_ref[...]
    k = jnp.dot(x, wk_ref[...], preferred_element_type=F32)
    ssq = _group_sumsq(k, bd_ref)
    kn = k * lax.rsqrt(ssq * (1.0 / head_dim) + QK_EPS) * gain_ref[...]
    for g in range(k.shape[1] // head_dim):
        k_ref[g] = kn[:, g * head_dim:(g + 1) * head_dim].astype(BF16)
    vt_ref[...] = lax.dot_general(wvt_ref[...], x, (((1,), (1,)), ((), ())),
                                  preferred_element_type=F32).astype(BF16)


def _kv_proj(n, w_k, w_vt, k_gain, bd, head_dim, seq):
    m, d = n.shape
    kvd = w_k.shape[1]
    nkv = kvd // head_dim
    bm = _tile(seq, 512)
    return pl.pallas_call(
        functools.partial(_kv_kernel, head_dim=head_dim),
        grid=(m // bm,),
        in_specs=[
            pl.BlockSpec((bm, d), lambda i: (i, 0)),
            pl.BlockSpec((d, kvd), lambda i: (0, 0)),
            pl.BlockSpec((kvd, d), lambda i: (0, 0)),
            pl.BlockSpec((1, kvd), lambda i: (0, 0)),
            pl.BlockSpec((kvd, kvd), lambda i: (0, 0)),
        ],
        out_specs=[pl.BlockSpec((nkv, bm, head_dim), lambda i: (0, i, 0)),
                   pl.BlockSpec((kvd, bm), lambda i: (0, i))],
        out_shape=[jax.ShapeDtypeStruct((nkv, m, head_dim), BF16),
                   jax.ShapeDtypeStruct((kvd, m), BF16)],
        compiler_params=_params(("arbitrary",), 48),
        name="kv_proj",
    )(n, w_k, w_vt, k_gain, bd)


def _q_attn_kernel(n_ref, wqt_ref, gain_ref, kc_ref, kp_ref, vc_ref, vp_ref, bias_ref, sink_ref,
                   o_ref, *, seq, head_dim):
    i = pl.program_id(0)
    bm = n_ref.shape[0]
    group = wqt_ref.shape[0] // head_dim
    qt = lax.dot_general(wqt_ref[...], n_ref[...], (((1,), (1,)), ((), ())), preferred_element_type=F32)
    gain = gain_ref[...]
    qn = []
    for hh in range(group):
        qh = qt[hh * head_dim:(hh + 1) * head_dim, :]
        rs = lax.rsqrt(jnp.sum(qh * qh, axis=0, keepdims=True) * (1.0 / head_dim) + QK_EPS)
        qn.append((qh * rs * gain[hh * head_dim:(hh + 1) * head_dim, :]).astype(BF16))
    sink = sink_ref[...]
    shape = (WINDOW, group * WINDOW)
    key_c = lax.broadcasted_iota(I32, shape, 0)
    query_i = lax.broadcasted_iota(I32, shape, 1) & (WINDOW - 1)
    in_cur = key_c <= query_i
    for qb in range(bm // WINDOW):
        r0 = qb * WINDOW
        q_all = jnp.concatenate([q[:, r0:r0 + WINDOW] for q in qn], axis=1)
        if qb == 0:
            k_prev, vt_prev = kp_ref[0], vp_ref[...]
        else:
            k_prev, vt_prev = kc_ref[0, r0 - WINDOW:r0, :], vc_ref[:, r0 - WINDOW:r0]
        k2 = jnp.concatenate([k_prev, kc_ref[0, r0:r0 + WINDOW, :]], axis=0)
        v2t = jnp.concatenate([vt_prev, vc_ref[:, r0:r0 + WINDOW]], axis=1)
        st = jnp.dot(k2, q_all, preferred_element_type=F32)
        first = ((i * bm + r0) % seq == 0).astype(I32)
        s = jnp.where(in_cur, st[WINDOW:], st[:WINDOW]) + bias_ref[first]
        mx = jnp.maximum(jnp.max(s, axis=0, keepdims=True), sink)
        p = jnp.exp(s - mx)
        denom = jnp.sum(p, axis=0, keepdims=True) + jnp.exp(sink - mx)
        pt = jnp.concatenate([jnp.where(in_cur, 0.0, p), jnp.where(in_cur, p, 0.0)], axis=0).astype(BF16)
        ot = jnp.dot(v2t, pt, preferred_element_type=F32) * (1.0 / denom)
        for pr in range(group // 2):
            blk = jnp.concatenate([ot[:, (2 * pr) * WINDOW:(2 * pr + 1) * WINDOW],
                                   ot[:, (2 * pr + 1) * WINDOW:(2 * pr + 2) * WINDOW]], axis=0)
            o_ref[r0:r0 + WINDOW, pr * 2 * head_dim:(pr + 1) * 2 * head_dim] = blk.T.astype(BF16)


def _q_attn(n, w_qt, q_gain, k, vt, bias, sink, layer, head_dim, seq):
    m, d = n.shape
    nkv = k.shape[0]
    gw = w_qt.shape[1] // nkv
    group = gw // head_dim
    assert head_dim == WINDOW // 2 and group % 2 == 0
    bm = q_gain.shape[1]
    nb = bm // WINDOW
    return pl.pallas_call(
        functools.partial(_q_attn_kernel, seq=seq, head_dim=head_dim),
        grid=(m // bm, nkv),
        in_specs=[
            pl.BlockSpec((bm, d), lambda i, g: (i, 0)),
            pl.BlockSpec((None, gw, d), lambda i, g: (layer, g, 0)),
            pl.BlockSpec((gw, bm), lambda i, g: (0, 0)),
            pl.BlockSpec((1, bm, head_dim), lambda i, g: (g, i, 0)),
            pl.BlockSpec((1, WINDOW, head_dim), lambda i, g: (g, jnp.maximum(i * nb - 1, 0), 0)),
            pl.BlockSpec((head_dim, bm), lambda i, g: (g, i)),
            pl.BlockSpec((head_dim, WINDOW), lambda i, g: (g, jnp.maximum(i * nb - 1, 0))),
            pl.BlockSpec((None, 2, WINDOW, group * WINDOW), lambda i, g: (g, 0, 0, 0)),
            pl.BlockSpec((None, 1, group * WINDOW), lambda i, g: (g, 0, 0)),
        ],
        out_specs=pl.BlockSpec((bm, gw), lambda i, g: (i, g)),
        out_shape=jax.ShapeDtypeStruct((m, w_qt.shape[1]), BF16),
        compiler_params=_params(("arbitrary", "arbitrary"), 48),
        name="q_attn",
    )(n, w_qt, q_gain, k, k, vt, vt, bias, sink)


def _attn_bias(n_heads, nkv):
    group = n_heads // nkv
    h = np.arange(1, n_heads + 1, dtype=np.float32)
    slopes = np.power(2.0, -8.0 * h / n_heads).astype(np.float32)
    c = np.arange(WINDOW)[:, None]
    qi = np.arange(WINDOW)[None, :]
    in_cur = c <= qi
    dist = np.where(in_cur, qi - c, WINDOW + qi - c).astype(np.float32)
    alibi = -slopes[:, None, None] * dist[None]
    first = np.where(in_cur[None], alibi, -np.inf).astype(np.float32)
    tables = np.stack([alibi, first], axis=1)
    tables = tables.reshape(nkv, group, 2, WINDOW, WINDOW).transpose(0, 2, 3, 1, 4)
    return jnp.asarray(tables.reshape(nkv, 2, WINDOW, group * WINDOW))


def _expert_ffn(x_ref, y_ref, wgu_ref, bgu_ref, wd_ref, bd_ref):
    lo, hi = _unpack_bf16_pair(x_ref[...])
    half = lo.shape[1]
    f = wd_ref.shape[1]
    x = jnp.concatenate([lo.astype(BF16), hi.astype(BF16)], axis=1)
    kc = x.shape[1] // MOE_IN_CHUNKS
    gu = bgu_ref[0]
    for ch in range(MOE_IN_CHUNKS):
        gu = gu + jnp.dot(x[:, ch * kc:(ch + 1) * kc], wgu_ref[0, ch * kc:(ch + 1) * kc, :].astype(BF16),
                          preferred_element_type=F32)
    x_glu = jnp.minimum(gu[:, :f], SWIGLU_LIMIT)
    x_lin = jnp.clip(gu[:, f:], -SWIGLU_LIMIT, SWIGLU_LIMIT)
    hid = (x_glu * jax.nn.sigmoid(SWIGLU_ALPHA * x_glu) * (x_lin + 1.0)).astype(BF16)
    cw = half // MOE_OUT_CHUNKS
    for ch in range(MOE_OUT_CHUNKS):
        a0, b0 = ch * cw, half + ch * cw
        ya = jnp.dot(hid, wd_ref[0, :, a0:a0 + cw].astype(BF16),
                     preferred_element_type=F32) + bd_ref[0, :, a0:a0 + cw]
        yb = jnp.dot(hid, wd_ref[0, :, b0:b0 + cw].astype(BF16),
                     preferred_element_type=F32) + bd_ref[0, :, b0:b0 + cw]
        y_ref[:, a0:a0 + cw] = _pack_bf16_pair(ya, yb)


def _dispatch_kernel(pos_ref, n_ref, xs_in, xs_out, sem, *, bt):
    def issue(r, carry):
        for k in range(TOP_K):
            p = pos_ref[0, 0, r * TOP_K + k]
            pltpu.make_async_copy(n_ref.at[pl.ds(r, 1)], xs_out.at[pl.ds(p, 1)], sem).start()
        return carry

    lax.fori_loop(0, bt, issue, 0)
    pltpu.make_async_copy(xs_in.at[pl.ds(0, bt * TOP_K)], xs_out.at[pl.ds(0, bt * TOP_K)], sem).wait()


def _dispatch(pos, n_packed, xs):
    m, dh = n_packed.shape
    bt = _tile(m, 512)
    return pl.pallas_call(
        functools.partial(_dispatch_kernel, bt=bt),
        grid=(m // bt,),
        in_specs=[
            pl.BlockSpec((1, 1, bt * TOP_K), lambda i: (i, 0, 0), memory_space=pltpu.SMEM),
            pl.BlockSpec((bt, dh), lambda i: (i, 0)),
            pl.BlockSpec(memory_space=pl.ANY),
        ],
        out_specs=pl.BlockSpec(memory_space=pl.ANY),
        out_shape=jax.ShapeDtypeStruct(xs.shape, xs.dtype),
        scratch_shapes=[pltpu.SemaphoreType.DMA(())],
        input_output_aliases={2: 0},
        compiler_params=_params(("arbitrary",), 24),
        name="moe_dispatch",
    )(pos.reshape(m // bt, 1, bt * TOP_K), n_packed, xs)


def _moe_kernel(te_ref, nv_ref, xs_ref, wgu_ref, bgu_ref, wd_ref, bd_ref, ys_ref):
    del te_ref
    valid = pl.program_id(0) < nv_ref[0]

    @pl.when(jnp.logical_not(valid))
    def _():
        ys_ref[...] = jnp.zeros_like(ys_ref)

    @pl.when(valid)
    def _():
        _expert_ffn(xs_ref, ys_ref, wgu_ref, bgu_ref, wd_ref, bd_ref)


def _moe_grouped(tile_expert, n_valid, xs, w_gu, b_gu, w_d, b_d, layer, tm):
    rows, dh = xs.shape
    _, ne, d, f2 = w_gu.shape
    f = f2 // 2
    assert dh % MOE_OUT_CHUNKS == 0
    by_expert = lambda j, te, nv: (layer, te[j], 0, 0)
    grid_spec = pltpu.PrefetchScalarGridSpec(
        num_scalar_prefetch=2,
        grid=(rows // tm,),
        in_specs=[
            pl.BlockSpec((tm, dh), lambda j, te, nv: (j, 0)),
            pl.BlockSpec((None, 1, d, f2), by_expert),
            pl.BlockSpec((None, 1, 1, f2), by_expert),
            pl.BlockSpec((None, 1, f, d), by_expert),
            pl.BlockSpec((None, 1, 1, d), by_expert),
        ],
        out_specs=pl.BlockSpec((tm, dh), lambda j, te, nv: (j, 0)),
    )
    return pl.pallas_call(
        _moe_kernel,
        grid_spec=grid_spec,
        out_shape=jax.ShapeDtypeStruct((rows, dh), U32),
        compiler_params=_params(("arbitrary",), 56),
        name="moe_grouped",
    )(tile_expert, n_valid, xs, w_gu, b_gu, w_d, b_d)


COMBINE_ROWS = 16
COMBINE_WORDS = 256


def _combine_kernel(*refs, n_norms):
    pos_ref, pos_next_ref, h_ref, wt_ref, g_ref = refs[:5]
    norm_refs = refs[5:5 + 2 * n_norms]
    ys_hbm = refs[5 + 2 * n_norms]
    o_ref = refs[6 + 2 * n_norms]
    n_refs = refs[7 + 2 * n_norms:7 + 3 * n_norms]
    buf0, buf1, sem = refs[7 + 3 * n_norms:]
    bc, d = h_ref.shape
    dh = d // 2
    rows, cw = COMBINE_ROWS, COMBINE_WORDS
    i = pl.program_id(0)
    last = pl.num_programs(0) - 1

    def gather_rows(p_ref, buf, s):
        for r in range(bc):
            for k in range(TOP_K):
                pltpu.make_async_copy(ys_hbm.at[pl.ds(p_ref[0, 0, r * TOP_K + k], 1)],
                                      buf.at[k, pl.ds(r, 1)], s).start()

    @pl.when(i == 0)
    def _():
        gather_rows(pos_ref, buf0, sem.at[0])

    def step(par):
        cur, nxt = (buf0, buf1) if par == 0 else (buf1, buf0)

        @pl.when(i < last)
        def _():
            gather_rows(pos_next_ref, nxt, sem.at[1 - par])

        for k in range(TOP_K):
            pltpu.make_async_copy(ys_hbm.at[pl.ds(0, bc)], cur.at[k], sem.at[par]).wait()
        lax.fori_loop(0, bc // rows, functools.partial(combine_rows, cur), 0)

    def combine_rows(y_ref, r, carry):
        rs = pl.ds(pl.multiple_of(r * rows, rows), rows)
        wt = wt_ref[rs, :]
        wk = [wt[:, k:k + 1] for k in range(TOP_K)]
        ssq = jnp.zeros((rows, LANES), F32)
        for c in range(dh // cw):
            acc_lo = acc_hi = None
            for k in range(TOP_K):
                lo, hi = _unpack_bf16_pair(y_ref[k, rs, c * cw:(c + 1) * cw])
                acc_lo = wk[k] * lo if acc_lo is None else acc_lo + wk[k] * lo
                acc_hi = wk[k] * hi if acc_hi is None else acc_hi + wk[k] * hi
            a0, b0 = c * cw, dh + c * cw
            ha = h_ref[rs, a0:a0 + cw] + g_ref[0, :, a0:a0 + cw] * acc_lo
            hb = h_ref[rs, b0:b0 + cw] + g_ref[0, :, b0:b0 + cw] * acc_hi
            o_ref[rs, a0:a0 + cw] = ha
            o_ref[rs, b0:b0 + cw] = hb
            if n_norms:
                for t in range(cw // LANES):
                    va = ha[:, t * LANES:(t + 1) * LANES]
                    vb = hb[:, t * LANES:(t + 1) * LANES]
                    ssq = ssq + va * va + vb * vb
        if n_norms:
            rsq = lax.rsqrt(jnp.sum(ssq, axis=-1, keepdims=True) * (1.0 / d) + NORM_EPS)
            for c in range(d // (2 * cw)):
                cs = slice(c * 2 * cw, (c + 1) * 2 * cw)
                xn = o_ref[rs, cs] * rsq
                for t in range(n_norms):
                    gs_ref, sh_ref = norm_refs[2 * t], norm_refs[2 * t + 1]
                    n_refs[t][rs, cs] = (xn * gs_ref[0, :, cs] + sh_ref[0, :, cs]).astype(BF16)
        return carry

    @pl.when(i % 2 == 0)
    def _():
        step(0)

    @pl.when(i % 2 == 1)
    def _():
        step(1)


def _combine(pos, ys, h, wt, gate, norms, seq):
    m, d = h.shape
    bsz = gate.shape[0]
    bc = _tile(seq, 256)
    nblk = m // bc
    pos_blocks = pos.reshape(nblk, 1, bc * TOP_K)
    pos_tile = lambda shift: pl.BlockSpec(
        (1, 1, bc * TOP_K), lambda i: (jnp.minimum(i + shift, nblk - 1), 0, 0), memory_space=pltpu.SMEM)
    per_batch = pl.BlockSpec((1, 1, d), lambda i: ((i * bc) // seq, 0, 0))
    row_tile = pl.BlockSpec((bc, d), lambda i: (i, 0))
    norm_args = [v.reshape(bsz, 1, d) for pair in norms for v in pair]
    outs = pl.pallas_call(
        functools.partial(_combine_kernel, n_norms=len(norms)),
        grid=(nblk,),
        in_specs=[
            pos_tile(0),
            pos_tile(1),
            row_tile,
            pl.BlockSpec((bc, LANES), lambda i: (i, 0)),
            per_batch,
        ] + [per_batch] * len(norm_args) + [pl.BlockSpec(memory_space=pl.ANY)],
        out_specs=[row_tile] * (1 + len(norms)),
        out_shape=[jax.ShapeDtypeStruct((m, d), F32)] + [jax.ShapeDtypeStruct((m, d), BF16)] * len(norms),
        scratch_shapes=[pltpu.VMEM((TOP_K, bc, d // 2), U32)] * 2 + [pltpu.SemaphoreType.DMA((2,))],
        compiler_params=_params(("arbitrary",), 56),
        name="moe_combine",
    )(pos_blocks, pos_blocks, h, wt, gate.reshape(bsz, 1, d), *norm_args, ys)
    return outs[0], list(outs[1:])


def _moe_layer(h, gain, scale, shift, gate, rw, rb, w_gu, b_gu, w_d, b_d, xs, next_norms, layer, seq, tm):
    m, d = h.shape
    ne = w_gu.shape[1]
    n_packed, route, wt, counts = _norm_router(h, gain, scale, shift, rw, rb, layer, seq)

    cnt = counts[0, :ne].astype(I32)
    padded = ((cnt + tm - 1) // tm) * tm
    ends = jnp.cumsum(padded)
    starts = ends - padded
    expert_ids = jnp.arange(ne, dtype=I32)
    idx = route[:, :TOP_K]
    pos = jnp.sum(jnp.where(idx[:, :, None] == expert_ids, starts, 0), axis=-1) + route[:, TOP_K:2 * TOP_K]
    n_tiles = xs.shape[0] // tm
    n_valid = ends[-1] // tm
    tile_id = jnp.arange(n_tiles, dtype=I32)
    tile_start = jnp.minimum(tile_id, n_valid - 1) * tm
    tile_expert = jnp.sum((ends[None, :] <= tile_start[:, None]).astype(I32), axis=1)

    xs = _dispatch(pos, n_packed, xs)
    ys = _moe_grouped(tile_expert, n_valid.reshape(1).astype(I32), xs, w_gu, b_gu, w_d, b_d, layer, tm)
    h, normed = _combine(pos, ys, h, wt, gate, next_norms, seq)
    return h, normed, xs


def kernel(x, c, norm_mix, norm_ffn, mod_down, mod_up, mod_b, a_w_in, a_conv, a_w_out, kv_norm, kv_mod_down, kv_mod_up, kv_mod_b, w_kv, k_norm, b_w_q, b_q_norm, b_sinks, b_w_o, router_w, router_b, moe_w_gu, moe_b_gu, moe_w_down, moe_b_down):
    bsz, seq, d = x.shape
    m = bsz * seq
    depth = norm_mix.shape[0]
    n_a = a_w_in.shape[0]
    head_dim = k_norm.shape[0]
    n_heads = b_sinks.shape[1]
    nkv = w_kv.shape[1] // (2 * head_dim)
    group = n_heads // nkv
    ne = router_w.shape[2]
    tm = _tile(m * TOP_K, 256)
    attn_bm = _tile(seq, 512)
    assert seq % WINDOW == 0 and n_a >= 1

    mods = _adaln(c, mod_down, mod_up, mod_b)
    kv_mods = _adaln(c, kv_mod_down[None], kv_mod_up[None], kv_mod_b[None])[0]

    w_in = a_w_in.astype(BF16)
    w_out = a_w_out.astype(BF16)
    w_qt = jnp.swapaxes(b_w_q, 1, 2).astype(BF16)
    w_o = b_w_o.astype(BF16)
    kvd = nkv * head_dim
    w_k = w_kv[:, :kvd].astype(BF16)
    w_vt = w_kv[:, kvd:].T.astype(BF16)
    w_gu = moe_w_gu
    w_d = moe_w_down
    b_gu = moe_b_gu[:, :, None, :]
    b_d = moe_b_down[:, :, None, :]
    rw = jnp.zeros((depth, d, LANES), BF16).at[:, :, :ne].set(router_w.astype(BF16))
    rb = jnp.full((depth, 1, LANES), NEG_BIG, F32).at[:, 0, :ne].set(router_b)

    gw = group * head_dim
    bd_kv = jnp.asarray(np.kron(np.eye(nkv, dtype=np.float32),
                                np.ones((head_dim, head_dim), np.float32)), BF16)
    attn_bias = _attn_bias(n_heads, nkv)

    h = x.reshape(m, d)
    xs = jnp.zeros((m * TOP_K + ne * tm, d // 2), U32)
    sh_kv, sc_kv = jnp.split(kv_mods, 2, axis=-1)
    k = vt = None
    sh_m, sc_m = mods[0, :, :d], mods[0, :, d:2 * d]
    n = _norm_mod(h, norm_mix[0], sc_m, sh_m, seq)
    n_kv = None
    for layer in range(depth):
        _, _, g_m, sh_f, sc_f, g_f = jnp.split(mods[layer], 6, axis=-1)
        if layer < n_a:
            y = _conv_in(n, w_in, a_conv, layer, seq)
            h = _matmul_res(y, w_out, h, g_m, layer, seq)
        else:
            if layer == n_a:
                k, vt = _kv_proj(n_kv, w_k, w_vt, jnp.tile(k_norm, nkv).reshape(1, kvd), bd_kv, head_dim, seq)
            i = layer - n_a
            q_gain = jnp.broadcast_to((jnp.tile(b_q_norm[i], group) * head_dim ** -0.5)[:, None], (gw, attn_bm))
            sink = jnp.repeat(b_sinks[i].reshape(nkv, group), WINDOW, axis=1).reshape(nkv, 1, group * WINDOW)
            a = _q_attn(n, w_qt, q_gain, k, vt, attn_bias, sink, i, head_dim, seq)
            h = _matmul_res(a, w_o, h, g_m, i, seq)
        next_norms = []
        if layer + 1 < depth:
            sh_n, sc_n = mods[layer + 1, :, :d], mods[layer + 1, :, d:2 * d]
            next_norms.append((norm_mix[layer + 1] * (1.0 + sc_n), sh_n))
            if layer + 1 == n_a:
                next_norms.append((kv_norm * (1.0 + sc_kv), sh_kv))
        h, normed, xs = _moe_layer(h, norm_ffn[layer], sc_f, sh_f, g_f, rw, rb, w_gu, b_gu, w_d, b_d, xs,
                                   next_norms, layer, seq, tm)
        if normed:
            n = normed[0]
            n_kv = normed[1] if len(normed) > 1 else None
    return h.reshape(bsz, seq, d)
```

```python
import functools

import numpy as np
import jax
import jax.numpy as jnp
from jax import lax
from jax.experimental import pallas as pl
from jax.experimental.pallas import tpu as pltpu

WINDOW = 128
TOP_K = 4
CONV_W = 3
NORM_EPS = 1e-5
QK_EPS = 1e-6
SWIGLU_ALPHA = 1.702
SWIGLU_LIMIT = 7.0
LANES = 128
SUBLANES = 8
MXU_DEPTH = 256
NEG_BIG = -1e30
MOE_OUT_CHUNKS = 2
MOE_IN_CHUNKS = 4
MIB = 1024 * 1024

F32 = jnp.float32
BF16 = jnp.bfloat16
U32 = jnp.uint32
I32 = jnp.int32
HI_MASK = np.uint32(0xFFFF0000)


def _tile(total, pref):
    t = min(total, pref)
    assert total % t == 0, (total, pref)
    return t


def _params(sem, vmem_mib):
    return pltpu.CompilerParams(dimension_semantics=sem, vmem_limit_bytes=vmem_mib * MIB)


def _pack_bf16_pair(a, b):
    lo = lax.bitcast_convert_type(a.astype(BF16).astype(F32), U32) >> 16
    hi = lax.bitcast_convert_type(b.astype(BF16).astype(F32), U32) & HI_MASK
    return lo | hi


def _unpack_bf16_pair(w):
    lo = lax.bitcast_convert_type(w << 16, F32)
    hi = lax.bitcast_convert_type(w & HI_MASK, F32)
    return lo, hi


def _adaln_kernel(c_ref, wd_ref, wu_ref, b_ref, o_ref, d_ref):
    @pl.when(pl.program_id(1) == 0)
    def _():
        c = c_ref[...]
        s = c * jax.nn.sigmoid(c)
        d_ref[...] = jnp.dot(s, wd_ref[0], preferred_element_type=F32, precision=lax.Precision.HIGHEST)

    o_ref[0] = jnp.dot(d_ref[...], wu_ref[0], preferred_element_type=F32,
                       precision=lax.Precision.HIGHEST) + b_ref[0]


def _adaln(c, w_down, w_up, b):
    nl, d, r = w_down.shape
    n = w_up.shape[2]
    bsz = c.shape[0]
    tn = _tile(n, min(d, 2048))
    return pl.pallas_call(
        _adaln_kernel,
        grid=(nl, n // tn),
        in_specs=[
            pl.BlockSpec((bsz, d), lambda l, j: (0, 0)),
            pl.BlockSpec((1, d, r), lambda l, j: (l, 0, 0)),
            pl.BlockSpec((1, r, tn), lambda l, j: (l, 0, j)),
            pl.BlockSpec((1, 1, tn), lambda l, j: (l, 0, j)),
        ],
        out_specs=pl.BlockSpec((1, bsz, tn), lambda l, j: (l, 0, j)),
        out_shape=jax.ShapeDtypeStruct((nl, bsz, n), F32),
        scratch_shapes=[pltpu.VMEM((bsz, r), F32)],
        compiler_params=_params(("arbitrary", "arbitrary"), 32),
        name="adaln",
    )(c, w_down, w_up, b.reshape(nl, 1, n))


def _norm_mod_value(h_ref, g_ref, sc_ref, sh_ref):
    x = h_ref[...]
    ms = jnp.mean(x * x, axis=-1, keepdims=True)
    y = x * lax.rsqrt(ms + NORM_EPS) * g_ref[...]
    return y * (1.0 + sc_ref[0]) + sh_ref[0]


def _norm_kernel(h_ref, g_ref, sc_ref, sh_ref, o_ref):
    o_ref[...] = _norm_mod_value(h_ref, g_ref, sc_ref, sh_ref).astype(BF16)


def _norm_mod(h, gain, scale, shift, seq):
    m, d = h.shape
    bsz = scale.shape[0]
    bm = _tile(seq, 256)
    row_b = lambda i: ((i * bm) // seq, 0, 0)
    return pl.pallas_call(
        _norm_kernel,
        grid=(m // bm,),
        in_specs=[
            pl.BlockSpec((bm, d), lambda i: (i, 0)),
            pl.BlockSpec((1, d), lambda i: (0, 0)),
            pl.BlockSpec((1, 1, d), row_b),
            pl.BlockSpec((1, 1, d), row_b),
        ],
        out_specs=pl.BlockSpec((bm, d), lambda i: (i, 0)),
        out_shape=jax.ShapeDtypeStruct((m, d), BF16),
        compiler_params=_params(("arbitrary",), 40),
        name="norm_mod",
    )(h, gain.reshape(1, d), scale.reshape(bsz, 1, d), shift.reshape(bsz, 1, d))


def _norm_router_kernel(h_ref, g_ref, sc_ref, sh_ref, rw_ref, rb_ref,
                        np_ref, ri_ref, wt_ref, cnt_ref, carry_ref):
    @pl.when(pl.program_id(0) == 0)
    def _():
        carry_ref[...] = jnp.zeros_like(carry_ref)

    y = _norm_mod_value(h_ref, g_ref, sc_ref, sh_ref)
    half = y.shape[1] // 2
    np_ref[...] = _pack_bf16_pair(y[:, :half], y[:, half:])
    logits = jnp.dot(y.astype(BF16), rw_ref[...], preferred_element_type=F32) + rb_ref[...]
    bm = logits.shape[0]
    lane = lax.broadcasted_iota(I32, logits.shape, 1)

    l = logits
    sels, vals, idxs = [], [], []
    for _ in range(TOP_K):
        mx = jnp.max(l, axis=-1, keepdims=True)
        idx = jnp.min(jnp.where(l == mx, lane, LANES), axis=-1, keepdims=True)
        sel = lane == idx
        l = jnp.where(sel, -jnp.inf, l)
        sels.append(sel)
        vals.append(mx)
        idxs.append(idx)
    exps = [jnp.exp(v - vals[0]) for v in vals]
    denom = exps[0]
    for e in exps[1:]:
        denom = denom + e

    mask = sels[0].astype(F32)
    for sel in sels[1:]:
        mask = mask + sel.astype(F32)
    rr = lax.broadcasted_iota(I32, (bm, bm), 0)
    cc = lax.broadcasted_iota(I32, (bm, bm), 1)
    tri = (cc < rr).astype(BF16)
    carry = carry_ref[0:1, :]
    prefix = jnp.dot(tri, mask.astype(BF16), preferred_element_type=F32) + carry

    ri = jnp.zeros(logits.shape, I32)
    wt = jnp.zeros(logits.shape, F32)
    for k in range(TOP_K):
        rank = jnp.sum(jnp.where(sels[k], prefix, 0.0), axis=-1, keepdims=True).astype(I32)
        ri = jnp.where(lane == k, idxs[k], ri)
        ri = jnp.where(lane == TOP_K + k, rank, ri)
        wt = jnp.where(lane == k, exps[k] / denom, wt)
    ri_ref[...] = ri
    wt_ref[...] = wt
    new_carry = jnp.broadcast_to(carry + jnp.sum(mask, axis=0, keepdims=True), carry_ref.shape)
    carry_ref[...] = new_carry
    cnt_ref[...] = new_carry


def _norm_router(h, gain, scale, shift, rw, rb, layer, seq):
    m, d = h.shape
    bsz = scale.shape[0]
    bm = _tile(seq, 256)
    row_b = lambda i: ((i * bm) // seq, 0, 0)
    return pl.pallas_call(
        _norm_router_kernel,
        grid=(m // bm,),
        in_specs=[
            pl.BlockSpec((bm, d), lambda i: (i, 0)),
            pl.BlockSpec((1, d), lambda i: (0, 0)),
            pl.BlockSpec((1, 1, d), row_b),
            pl.BlockSpec((1, 1, d), row_b),
            pl.BlockSpec((None, d, LANES), lambda i: (layer, 0, 0)),
            pl.BlockSpec((None, 1, LANES), lambda i: (layer, 0, 0)),
        ],
        out_specs=[
            pl.BlockSpec((bm, d // 2), lambda i: (i, 0)),
            pl.BlockSpec((bm, LANES), lambda i: (i, 0)),
            pl.BlockSpec((bm, LANES), lambda i: (i, 0)),
            pl.BlockSpec((SUBLANES, LANES), lambda i: (0, 0)),
        ],
        out_shape=[
            jax.ShapeDtypeStruct((m, d // 2), U32),
            jax.ShapeDtypeStruct((m, LANES), I32),
            jax.ShapeDtypeStruct((m, LANES), F32),
            jax.ShapeDtypeStruct((SUBLANES, LANES), F32),
        ],
        scratch_shapes=[pltpu.VMEM((SUBLANES, LANES), F32)],
        compiler_params=_params(("arbitrary",), 40),
        name="norm_router",
    )(h, gain.reshape(1, d), scale.reshape(bsz, 1, d), shift.reshape(bsz, 1, d), rw, rb)


def _conv_in_kernel(n_ref, wb_ref, wc_ref, wv_ref, cw_ref, y_ref, tail_ref, *, seq):
    i = pl.program_id(0)
    j = pl.program_id(1)
    x = n_ref[...]
    bm = x.shape[0]
    b_gate = jnp.dot(x, wb_ref[...], preferred_element_type=F32)
    c_gate = jnp.dot(x, wc_ref[...], preferred_element_type=F32)
    v = jnp.dot(x, wv_ref[...], preferred_element_type=F32)
    u = c_gate * v
    prev = jnp.where((i * bm) % seq == 0, 0.0, tail_ref[j])
    tail_ref[j] = u[bm - SUBLANES:, :]
    u1 = pltpu.roll(u, 1, axis=0)
    u2 = pltpu.roll(u, 2, axis=0)
    row = lax.broadcasted_iota(I32, prev.shape, 0)
    head1 = jnp.where(row < 1, pltpu.roll(prev, 1, axis=0), u1[:SUBLANES])
    head2 = jnp.where(row < 2, pltpu.roll(prev, 2, axis=0), u2[:SUBLANES])
    u1 = jnp.concatenate([head1, u1[SUBLANES:]], axis=0)
    u2 = jnp.concatenate([head2, u2[SUBLANES:]], axis=0)
    cw = cw_ref[...]
    conv = cw[0:1, :] * u2 + cw[1:2, :] * u1 + cw[2:3, :] * u
    y_ref[...] = (b_gate * conv).astype(BF16)


def _conv_in(n, w_in, conv_w, layer, seq):
    m, d = n.shape
    bm = _tile(seq, 1024)
    bn = _tile(d, 256)
    nj = d // bn
    return pl.pallas_call(
        functools.partial(_conv_in_kernel, seq=seq),
        grid=(m // bm, nj),
        in_specs=[
            pl.BlockSpec((bm, d), lambda i, j: (i, 0)),
            pl.BlockSpec((None, d, bn), lambda i, j: (layer, 0, j)),
            pl.BlockSpec((None, d, bn), lambda i, j: (layer, 0, nj + j)),
            pl.BlockSpec((None, d, bn), lambda i, j: (layer, 0, 2 * nj + j)),
            pl.BlockSpec((None, CONV_W, bn), lambda i, j: (layer, 0, j)),
        ],
        out_specs=pl.BlockSpec((bm, bn), lambda i, j: (i, j)),
        out_shape=jax.ShapeDtypeStruct((m, d), BF16),
        scratch_shapes=[pltpu.VMEM((nj, SUBLANES, bn), F32)],
        compiler_params=_params(("arbitrary", "arbitrary"), 48),
        name="conv_in",
    )(n, w_in, w_in, w_in, conv_w)


def _matmul_res_kernel(a_ref, w_ref, h_ref, g_ref, o_ref):
    acc = jnp.dot(a_ref[...], w_ref[...], preferred_element_type=F32)
    o_ref[...] = h_ref[...] + g_ref[0] * acc


def _matmul_res(a, w, h, gate, layer, seq):
    m, k = a.shape
    n = w.shape[2]
    bsz = gate.shape[0]
    bm = _tile(seq, 1024)
    bn = _tile(n, 512)
    return pl.pallas_call(
        _matmul_res_kernel,
        grid=(m // bm, n // bn),
        in_specs=[
            pl.BlockSpec((bm, k), lambda i, j: (i, 0)),
            pl.BlockSpec((None, k, bn), lambda i, j: (layer, 0, j)),
            pl.BlockSpec((bm, bn), lambda i, j: (i, j)),
            pl.BlockSpec((1, 1, bn), lambda i, j: ((i * bm) // seq, 0, j)),
        ],
        out_specs=pl.BlockSpec((bm, bn), lambda i, j: (i, j)),
        out_shape=jax.ShapeDtypeStruct((m, n), F32),
        compiler_params=_params(("arbitrary", "arbitrary"), 48),
        name="matmul_res",
    )(a, w, h, gate.reshape(bsz, 1, n))


def _group_sumsq(t, bd_ref):
    tt = t * t
    hi = tt.astype(BF16)
    lo = (tt - hi.astype(F32)).astype(BF16)
    bd = bd_ref[...]
    return jnp.dot(hi, bd, preferred_element_type=F32) + jnp.dot(lo, bd, preferred_element_type=F32)


def _kv_kernel(n_ref, wk_ref, wvt_ref, gain_ref, bd_ref, k_ref, vt_ref, *, head_dim):
    x = n_ref[...]
    k = jnp.dot(x, wk_ref[...], preferred_element_type=F32)
    ssq = _group_sumsq(k, bd_ref)
    kn = k * lax.rsqrt(ssq * (1.0 / head_dim) + QK_EPS) * gain_ref[...]
    for g in range(k.shape[1] // head_dim):
        k_ref[g] = kn[:, g * head_dim:(g + 1) * head_dim].astype(BF16)
    vt_ref[...] = lax.dot_general(wvt_ref[...], x, (((1,), (1,)), ((), ())),
                                  preferred_element_type=F32).astype(BF16)


def _kv_proj(n, w_k, w_vt, k_gain, bd, head_dim, seq):
    m, d = n.shape
    kvd = w_k.shape[1]
    nkv = kvd // head_dim
    bm = _tile(seq, 512)
    return pl.pallas_call(
        functools.partial(_kv_kernel, head_dim=head_dim),
        grid=(m // bm,),
        in_specs=[
            pl.BlockSpec((bm, d), lambda i: (i, 0)),
            pl.BlockSpec((d, kvd), lambda i: (0, 0)),
            pl.BlockSpec((kvd, d), lambda i: (0, 0)),
            pl.BlockSpec((1, kvd), lambda i: (0, 0)),
            pl.BlockSpec((kvd, kvd), lambda i: (0, 0)),
        ],
        out_specs=[pl.BlockSpec((nkv, bm, head_dim), lambda i: (0, i, 0)),
                   pl.BlockSpec((kvd, bm), lambda i: (0, i))],
        out_shape=[jax.ShapeDtypeStruct((nkv, m, head_dim), BF16),
                   jax.ShapeDtypeStruct((kvd, m), BF16)],
        compiler_params=_params(("arbitrary",), 48),
        name="kv_proj",
    )(n, w_k, w_vt, k_gain, bd)


def _q_attn_kernel(n_ref, wqt_ref, gain_ref, kc_ref, kp_ref, vc_ref, vp_ref, bias_ref, sink_ref,
                   o_ref, *, seq, head_dim):
    i = pl.program_id(0)
    bm = n_ref.shape[0]
    group = wqt_ref.shape[0] // head_dim
    qt = lax.dot_general(wqt_ref[...], n_ref[...], (((1,), (1,)), ((), ())), preferred_element_type=F32)
    gain = gain_ref[...]
    qn = []
    for hh in range(group):
        qh = qt[hh * head_dim:(hh + 1) * head_dim, :]
        rs = lax.rsqrt(jnp.sum(qh * qh, axis=0, keepdims=True) * (1.0 / head_dim) + QK_EPS)
        qn.append((qh * rs * gain[hh * head_dim:(hh + 1) * head_dim, :]).astype(BF16))
    sink = sink_ref[...]
    shape = (WINDOW, group * WINDOW)
    key_c = lax.broadcasted_iota(I32, shape, 0)
    query_i = lax.broadcasted_iota(I32, shape, 1) & (WINDOW - 1)
    in_cur = key_c <= query_i
    for qb in range(bm // WINDOW):
        r0 = qb * WINDOW
        q_all = jnp.concatenate([q[:, r0:r0 + WINDOW] for q in qn], axis=1)
        if qb == 0:
            k_prev, vt_prev = kp_ref[0], vp_ref[...]
        else:
            k_prev, vt_prev = kc_ref[0, r0 - WINDOW:r0, :], vc_ref[:, r0 - WINDOW:r0]
        k2 = jnp.concatenate([k_prev, kc_ref[0, r0:r0 + WINDOW, :]], axis=0)
        v2t = jnp.concatenate([vt_prev, vc_ref[:, r0:r0 + WINDOW]], axis=1)
        st = jnp.dot(k2, q_all, preferred_element_type=F32)
        first = ((i * bm + r0) % seq == 0).astype(I32)
        s = jnp.where(in_cur, st[WINDOW:], st[:WINDOW]) + bias_ref[first]
        mx = jnp.maximum(jnp.max(s, axis=0, keepdims=True), sink)
        p = jnp.exp(s - mx)
        denom = jnp.sum(p, axis=0, keepdims=True) + jnp.exp(sink - mx)
        pt = jnp.concatenate([jnp.where(in_cur, 0.0, p), jnp.where(in_cur, p, 0.0)], axis=0).astype(BF16)
        ot = jnp.dot(v2t, pt, preferred_element_type=F32) * (1.0 / denom)
        for pr in range(group // 2):
            blk = jnp.concatenate([ot[:, (2 * pr) * WINDOW:(2 * pr + 1) * WINDOW],
                                   ot[:, (2 * pr + 1) * WINDOW:(2 * pr + 2) * WINDOW]], axis=0)
            o_ref[r0:r0 + WINDOW, pr * 2 * head_dim:(pr + 1) * 2 * head_dim] = blk.T.astype(BF16)


def _q_attn(n, w_qt, q_gain, k, vt, bias, sink, layer, head_dim, seq):
    m, d = n.shape
    nkv = k.shape[0]
    gw = w_qt.shape[1] // nkv
    group = gw // head_dim
    assert head_dim == WINDOW // 2 and group % 2 == 0
    bm = q_gain.shape[1]
    nb = bm // WINDOW
    return pl.pallas_call(
        functools.partial(_q_attn_kernel, seq=seq, head_dim=head_dim),
        grid=(m // bm, nkv),
        in_specs=[
            pl.BlockSpec((bm, d), lambda i, g: (i, 0)),
            pl.BlockSpec((None, gw, d), lambda i, g: (layer, g, 0)),
            pl.BlockSpec((gw, bm), lambda i, g: (0, 0)),
            pl.BlockSpec((1, bm, head_dim), lambda i, g: (g, i, 0)),
            pl.BlockSpec((1, WINDOW, head_dim), lambda i, g: (g, jnp.maximum(i * nb - 1, 0), 0)),
            pl.BlockSpec((head_dim, bm), lambda i, g: (g, i)),
            pl.BlockSpec((head_dim, WINDOW), lambda i, g: (g, jnp.maximum(i * nb - 1, 0))),
            pl.BlockSpec((None, 2, WINDOW, group * WINDOW), lambda i, g: (g, 0, 0, 0)),
            pl.BlockSpec((None, 1, group * WINDOW), lambda i, g: (g, 0, 0)),
        ],
        out_specs=pl.BlockSpec((bm, gw), lambda i, g: (i, g)),
        out_shape=jax.ShapeDtypeStruct((m, w_qt.shape[1]), BF16),
        compiler_params=_params(("arbitrary", "arbitrary"), 48),
        name="q_attn",
    )(n, w_qt, q_gain, k, k, vt, vt, bias, sink)


def _attn_bias(n_heads, nkv):
    group = n_heads // nkv
    h = np.arange(1, n_heads + 1, dtype=np.float32)
    slopes = np.power(2.0, -8.0 * h / n_heads).astype(np.float32)
    c = np.arange(WINDOW)[:, None]
    qi = np.arange(WINDOW)[None, :]
    in_cur = c <= qi
    dist = np.where(in_cur, qi - c, WINDOW + qi - c).astype(np.float32)
    alibi = -slopes[:, None, None] * dist[None]
    first = np.where(in_cur[None], alibi, -np.inf).astype(np.float32)
    tables = np.stack([alibi, first], axis=1)
    tables = tables.reshape(nkv, group, 2, WINDOW, WINDOW).transpose(0, 2, 3, 1, 4)
    return jnp.asarray(tables.reshape(nkv, 2, WINDOW, group * WINDOW))


def _expert_ffn(x_ref, y_ref, wgu_ref, bgu_ref, wd_ref, bd_ref):
    lo, hi = _unpack_bf16_pair(x_ref[...])
    half = lo.shape[1]
    f = wd_ref.shape[1]
    x = jnp.concatenate([lo.astype(BF16), hi.astype(BF16)], axis=1)
    kc = x.shape[1] // MOE_IN_CHUNKS
    f_split = [(0, min(f, MXU_DEPTH))] + ([(MXU_DEPTH, f)] if f > MXU_DEPTH else [])
    hids = []
    for f0, f1 in f_split:
        gl = jnp.concatenate([bgu_ref[0, :, f0:f1], bgu_ref[0, :, f + f0:f + f1]], axis=1)
        for ch in range(MOE_IN_CHUNKS):
            k0 = ch * kc
            w = jnp.concatenate([wgu_ref[0, k0:k0 + kc, f0:f1], wgu_ref[0, k0:k0 + kc, f + f0:f + f1]], axis=1)
            gl = gl + jnp.dot(x[:, k0:k0 + kc], w.astype(BF16), preferred_element_type=F32)
        glu = jnp.minimum(gl[:, :f1 - f0], SWIGLU_LIMIT)
        lin = jnp.clip(gl[:, f1 - f0:], -SWIGLU_LIMIT, SWIGLU_LIMIT)
        hids.append((glu * jax.nn.sigmoid(SWIGLU_ALPHA * glu) * (lin + 1.0)).astype(BF16))
    cw = half // MOE_OUT_CHUNKS
    for ch in range(MOE_OUT_CHUNKS):
        a0, b0 = ch * cw, half + ch * cw
        ya = bd_ref[0, :, a0:a0 + cw]
        yb = bd_ref[0, :, b0:b0 + cw]
        for (f0, f1), hid in zip(f_split, hids):
            ya = ya + jnp.dot(hid, wd_ref[0, f0:f1, a0:a0 + cw].astype(BF16), preferred_element_type=F32)
            yb = yb + jnp.dot(hid, wd_ref[0, f0:f1, b0:b0 + cw].astype(BF16), preferred_element_type=F32)
        y_ref[:, a0:a0 + cw] = _pack_bf16_pair(ya, yb)


def _dispatch_kernel(pos_ref, n_ref, xs_in, xs_out, sem, *, bt):
    for r in range(bt):
        for k in range(TOP_K):
            p = pos_ref[0, 0, r * TOP_K + k]
            pltpu.make_async_copy(n_ref.at[pl.ds(r, 1)], xs_out.at[pl.ds(p, 1)], sem).start()
    pltpu.make_async_copy(xs_in.at[pl.ds(0, bt * TOP_K)], xs_out.at[pl.ds(0, bt * TOP_K)], sem).wait()


def _dispatch(pos, n_packed, xs):
    m, dh = n_packed.shape
    bt = _tile(m, 512)
    return pl.pallas_call(
        functools.partial(_dispatch_kernel, bt=bt),
        grid=(m // bt,),
        in_specs=[
            pl.BlockSpec((1, 1, bt * TOP_K), lambda i: (i, 0, 0), memory_space=pltpu.SMEM),
            pl.BlockSpec((bt, dh), lambda i: (i, 0)),
            pl.BlockSpec(memory_space=pl.ANY),
        ],
        out_specs=pl.BlockSpec(memory_space=pl.ANY),
        out_shape=jax.ShapeDtypeStruct(xs.shape, xs.dtype),
        scratch_shapes=[pltpu.SemaphoreType.DMA(())],
        input_output_aliases={2: 0},
        compiler_params=_params(("arbitrary",), 24),
        name="moe_dispatch",
    )(pos.reshape(m // bt, 1, bt * TOP_K), n_packed, xs)


def _moe_kernel(te_ref, nv_ref, xs_ref, wgu_ref, bgu_ref, wd_ref, bd_ref, ys_ref):
    del te_ref
    valid = pl.program_id(0) < nv_ref[0]

    @pl.when(jnp.logical_not(valid))
    def _():
        ys_ref[...] = jnp.zeros_like(ys_ref)

    @pl.when(valid)
    def _():
        _expert_ffn(xs_ref, ys_ref, wgu_ref, bgu_ref, wd_ref, bd_ref)


def _moe_grouped(tile_expert, n_valid, xs, w_gu, b_gu, w_d, b_d, layer, tm):
    rows, dh = xs.shape
    _, ne, d, f2 = w_gu.shape
    f = f2 // 2
    assert dh % MOE_OUT_CHUNKS == 0
    by_expert = lambda j, te, nv: (layer, te[j], 0, 0)
    grid_spec = pltpu.PrefetchScalarGridSpec(
        num_scalar_prefetch=2,
        grid=(rows // tm,),
        in_specs=[
            pl.BlockSpec((tm, dh), lambda j, te, nv: (j, 0)),
            pl.BlockSpec((None, 1, d, f2), by_expert),
            pl.BlockSpec((None, 1, 1, f2), by_expert),
            pl.BlockSpec((None, 1, f, d), by_expert),
            pl.BlockSpec((None, 1, 1, d), by_expert),
        ],
        out_specs=pl.BlockSpec((tm, dh), lambda j, te, nv: (j, 0)),
    )
    return pl.pallas_call(
        _moe_kernel,
        grid_spec=grid_spec,
        out_shape=jax.ShapeDtypeStruct((rows, dh), U32),
        compiler_params=_params(("arbitrary",), 56),
        name="moe_grouped",
    )(tile_expert, n_valid, xs, w_gu, b_gu, w_d, b_d)


COMBINE_ROWS = 16
COMBINE_WORDS = 256


def _combine_kernel(*refs, n_norms):
    pos_ref, pos_next_ref, h_ref, wt_ref, g_ref = refs[:5]
    norm_refs = refs[5:5 + 2 * n_norms]
    ys_hbm = refs[5 + 2 * n_norms]
    o_ref = refs[6 + 2 * n_norms]
    n_refs = refs[7 + 2 * n_norms:7 + 3 * n_norms]
    buf0, buf1, sem = refs[7 + 3 * n_norms:]
    bc, d = h_ref.shape
    dh = d // 2
    rows, cw = COMBINE_ROWS, COMBINE_WORDS
    i = pl.program_id(0)
    last = pl.num_programs(0) - 1

    def gather_rows(p_ref, buf, s):
        for r in range(bc):
            for k in range(TOP_K):
                pltpu.make_async_copy(ys_hbm.at[pl.ds(p_ref[0, 0, r * TOP_K + k], 1)],
                                      buf.at[k, pl.ds(r, 1)], s).start()

    @pl.when(i == 0)
    def _():
        gather_rows(pos_ref, buf0, sem.at[0])

    def step(par):
        cur, nxt = (buf0, buf1) if par == 0 else (buf1, buf0)

        @pl.when(i < last)
        def _():
            gather_rows(pos_next_ref, nxt, sem.at[1 - par])

        for k in range(TOP_K):
            pltpu.make_async_copy(ys_hbm.at[pl.ds(0, bc)], cur.at[k], sem.at[par]).wait()
        lax.fori_loop(0, bc // rows, functools.partial(combine_rows, cur), 0)

    def combine_rows(y_ref, r, carry):
        rs = pl.ds(pl.multiple_of(r * rows, rows), rows)
        wt = wt_ref[rs, :]
        wk = [wt[:, k:k + 1] for k in range(TOP_K)]
        ssq = jnp.zeros((rows, LANES), F32)
        for c in range(dh // cw):
            acc_lo = acc_hi = None
            for k in range(TOP_K):
                lo, hi = _unpack_bf16_pair(y_ref[k, rs, c * cw:(c + 1) * cw])
                acc_lo = wk[k] * lo if acc_lo is None else acc_lo + wk[k] * lo
                acc_hi = wk[k] * hi if acc_hi is None else acc_hi + wk[k] * hi
            a0, b0 = c * cw, dh + c * cw
            ha = h_ref[rs, a0:a0 + cw] + g_ref[0, :, a0:a0 + cw] * acc_lo
            hb = h_ref[rs, b0:b0 + cw] + g_ref[0, :, b0:b0 + cw] * acc_hi
            o_ref[rs, a0:a0 + cw] = ha
            o_ref[rs, b0:b0 + cw] = hb
            if n_norms:
                for t in range(cw // LANES):
                    va = ha[:, t * LANES:(t + 1) * LANES]
                    vb = hb[:, t * LANES:(t + 1) * LANES]
                    ssq = ssq + va * va + vb * vb
        if n_norms:
            rsq = lax.rsqrt(jnp.sum(ssq, axis=-1, keepdims=True) * (1.0 / d) + NORM_EPS)
            for c in range(d // (2 * cw)):
                cs = slice(c * 2 * cw, (c + 1) * 2 * cw)
                xn = o_ref[rs, cs] * rsq
                for t in range(n_norms):
                    gs_ref, sh_ref = norm_refs[2 * t], norm_refs[2 * t + 1]
                    n_refs[t][rs, cs] = (xn * gs_ref[0, :, cs] + sh_ref[0, :, cs]).astype(BF16)
        return carry

    @pl.when(i % 2 == 0)
    def _():
        step(0)

    @pl.when(i % 2 == 1)
    def _():
        step(1)


def _combine(pos, ys, h, wt, gate, norms, seq):
    m, d = h.shape
    bsz = gate.shape[0]
    bc = _tile(seq, 256)
    nblk = m // bc
    pos_blocks = pos.reshape(nblk, 1, bc * TOP_K)
    pos_tile = lambda shift: pl.BlockSpec(
        (1, 1, bc * TOP_K), lambda i: (jnp.minimum(i + shift, nblk - 1), 0, 0), memory_space=pltpu.SMEM)
    per_batch = pl.BlockSpec((1, 1, d), lambda i: ((i * bc) // seq, 0, 0))
    row_tile = pl.BlockSpec((bc, d), lambda i: (i, 0))
    norm_args = [v.reshape(bsz, 1, d) for pair in norms for v in pair]
    outs = pl.pallas_call(
        functools.partial(_combine_kernel, n_norms=len(norms)),
        grid=(nblk,),
        in_specs=[
            pos_tile(0),
            pos_tile(1),
            row_tile,
            pl.BlockSpec((bc, LANES), lambda i: (i, 0)),
            per_batch,
        ] + [per_batch] * len(norm_args) + [pl.BlockSpec(memory_space=pl.ANY)],
        out_specs=[row_tile] * (1 + len(norms)),
        out_shape=[jax.ShapeDtypeStruct((m, d), F32)] + [jax.ShapeDtypeStruct((m, d), BF16)] * len(norms),
        scratch_shapes=[pltpu.VMEM((TOP_K, bc, d // 2), U32)] * 2 + [pltpu.SemaphoreType.DMA((2,))],
        compiler_params=_params(("arbitrary",), 56),
        name="moe_combine",
    )(pos_blocks, pos_blocks, h, wt, gate.reshape(bsz, 1, d), *norm_args, ys)
    return outs[0], list(outs[1:])


def _moe_layer(h, gain, scale, shift, gate, rw, rb, w_gu, b_gu, w_d, b_d, xs, next_norms, layer, seq, tm):
    m, d = h.shape
    ne = w_gu.shape[1]
    n_packed, route, wt, counts = _norm_router(h, gain, scale, shift, rw, rb, layer, seq)

    cnt = counts[0, :ne].astype(I32)
    padded = ((cnt + tm - 1) // tm) * tm
    ends = jnp.cumsum(padded)
    starts = ends - padded
    expert_ids = jnp.arange(ne, dtype=I32)
    idx = route[:, :TOP_K]
    pos = jnp.sum(jnp.where(idx[:, :, None] == expert_ids, starts, 0), axis=-1) + route[:, TOP_K:2 * TOP_K]
    n_tiles = xs.shape[0] // tm
    n_valid = ends[-1] // tm
    tile_id = jnp.arange(n_tiles, dtype=I32)
    tile_start = jnp.minimum(tile_id, n_valid - 1) * tm
    tile_expert = jnp.sum((ends[None, :] <= tile_start[:, None]).astype(I32), axis=1)

    xs = _dispatch(pos, n_packed, xs)
    ys = _moe_grouped(tile_expert, n_valid.reshape(1).astype(I32), xs, w_gu, b_gu, w_d, b_d, layer, tm)
    h, normed = _combine(pos, ys, h, wt, gate, next_norms, seq)
    return h, normed, xs


def kernel(x, c, norm_mix, norm_ffn, mod_down, mod_up, mod_b, a_w_in, a_conv, a_w_out, kv_norm, kv_mod_down, kv_mod_up, kv_mod_b, w_kv, k_norm, b_w_q, b_q_norm, b_sinks, b_w_o, router_w, router_b, moe_w_gu, moe_b_gu, moe_w_down, moe_b_down):
    bsz, seq, d = x.shape
    m = bsz * seq
    depth = norm_mix.shape[0]
    n_a = a_w_in.shape[0]
    head_dim = k_norm.shape[0]
    n_heads = b_sinks.shape[1]
    nkv = w_kv.shape[1] // (2 * head_dim)
    group = n_heads // nkv
    ne = router_w.shape[2]
    tm = _tile(m * TOP_K, 256)
    attn_bm = _tile(seq, 512)
    assert seq % WINDOW == 0 and n_a >= 1

    mods = _adaln(c, mod_down, mod_up, mod_b)
    kv_mods = _adaln(c, kv_mod_down[None], kv_mod_up[None], kv_mod_b[None])[0]

    w_in = a_w_in.astype(BF16)
    w_out = a_w_out.astype(BF16)
    w_qt = jnp.swapaxes(b_w_q, 1, 2).astype(BF16)
    w_o = b_w_o.astype(BF16)
    kvd = nkv * head_dim
    w_k = w_kv[:, :kvd].astype(BF16)
    w_vt = w_kv[:, kvd:].T.astype(BF16)
    w_gu = moe_w_gu
    w_d = moe_w_down
    b_gu = moe_b_gu[:, :, None, :]
    b_d = moe_b_down[:, :, None, :]
    rw = jnp.zeros((depth, d, LANES), BF16).at[:, :, :ne].set(router_w.astype(BF16))
    rb = jnp.full((depth, 1, LANES), NEG_BIG, F32).at[:, 0, :ne].set(router_b)

    gw = group * head_dim
    bd_kv = jnp.asarray(np.kron(np.eye(nkv, dtype=np.float32),
                                np.ones((head_dim, head_dim), np.float32)), BF16)
    attn_bias = _attn_bias(n_heads, nkv)

    h = x.reshape(m, d)
    xs = jnp.zeros((m * TOP_K + ne * tm, d // 2), U32)
    sh_kv, sc_kv = jnp.split(kv_mods, 2, axis=-1)
    k = vt = None
    sh_m, sc_m = mods[0, :, :d], mods[0, :, d:2 * d]
    n = _norm_mod(h, norm_mix[0], sc_m, sh_m, seq)
    n_kv = None
    for layer in range(depth):
        _, _, g_m, sh_f, sc_f, g_f = jnp.split(mods[layer], 6, axis=-1)
        if layer < n_a:
            y = _conv_in(n, w_in, a_conv, layer, seq)
            h = _matmul_res(y, w_out, h, g_m, layer, seq)
        else:
            if layer == n_a:
                k, vt = _kv_proj(n_kv, w_k, w_vt, jnp.tile(k_norm, nkv).reshape(1, kvd), bd_kv, head_dim, seq)
            i = layer - n_a
            q_gain = jnp.broadcast_to((jnp.tile(b_q_norm[i], group) * head_dim ** -0.5)[:, None], (gw, attn_bm))
            sink = jnp.repeat(b_sinks[i].reshape(nkv, group), WINDOW, axis=1).reshape(nkv, 1, group * WINDOW)
            a = _q_attn(n, w_qt, q_gain, k, vt, attn_bias, sink, i, head_dim, seq)
            h = _matmul_res(a, w_o, h, g_m, i, seq)
        next_norms = []
        if layer + 1 < depth:
            sh_n, sc_n = mods[layer + 1, :, :d], mods[layer + 1, :, d:2 * d]
            next_norms.append((norm_mix[layer + 1] * (1.0 + sc_n), sh_n))
            if layer + 1 == n_a:
                next_norms.append((kv_norm * (1.0 + sc_kv), sh_kv))
        h, normed, xs = _moe_layer(h, norm_ffn[layer], sc_f, sh_f, g_f, rw, rb, w_gu, b_gu, w_d, b_d, xs,
                                   next_norms, layer, seq, tm)
        if normed:
            n = normed[0]
            n_kv = normed[1] if len(normed) > 1 else None
    return h.reshape(bsz, seq, d)
```

```python
import functools

import numpy as np
import jax
import jax.numpy as jnp
from jax import lax
from jax.experimental import pallas as pl
from jax.experimental.pallas import tpu as pltpu

WINDOW = 128
TOP_K = 4
CONV_W = 3
NORM_EPS = 1e-5
QK_EPS = 1e-6
SWIGLU_ALPHA = 1.702
SWIGLU_LIMIT = 7.0
LANES = 128
SUBLANES = 8
MXU_DEPTH = 256
NEG_BIG = -1e30
MOE_OUT_CHUNKS = 2
MOE_IN_CHUNKS = 4
MIB = 1024 * 1024

F32 = jnp.float32
BF16 = jnp.bfloat16
U32 = jnp.uint32
I32 = jnp.int32
HI_MASK = np.uint32(0xFFFF0000)


def _tile(total, pref):
    t = min(total, pref)
    assert total % t == 0, (total, pref)
    return t


def _params(sem, vmem_mib):
    return pltpu.CompilerParams(dimension_semantics=sem, vmem_limit_bytes=vmem_mib * MIB)


def _pack_bf16_pair(a, b):
    lo = lax.bitcast_convert_type(a.astype(BF16).astype(F32), U32) >> 16
    hi = lax.bitcast_convert_type(b.astype(BF16).astype(F32), U32) & HI_MASK
    return lo | hi


def _unpack_bf16_pair(w):
    lo = lax.bitcast_convert_type(w << 16, F32)
    hi = lax.bitcast_convert_type(w & HI_MASK, F32)
    return lo, hi


def _adaln_kernel(c_ref, wd_ref, wu_ref, b_ref, o_ref, d_ref):
    @pl.when(pl.program_id(1) == 0)
    def _():
        c = c_ref[...]
        s = c * jax.nn.sigmoid(c)
        d_ref[...] = jnp.dot(s, wd_ref[0], preferred_element_type=F32, precision=lax.Precision.HIGHEST)

    o_ref[0] = jnp.dot(d_ref[...], wu_ref[0], preferred_element_type=F32,
                       precision=lax.Precision.HIGHEST) + b_ref[0]


def _adaln(c, w_down, w_up, b):
    nl, d, r = w_down.shape
    n = w_up.shape[2]
    bsz = c.shape[0]
    tn = _tile(n, min(d, 2048))
    return pl.pallas_call(
        _adaln_kernel,
        grid=(nl, n // tn),
        in_specs=[
            pl.BlockSpec((bsz, d), lambda l, j: (0, 0)),
            pl.BlockSpec((1, d, r), lambda l, j: (l, 0, 0)),
            pl.BlockSpec((1, r, tn), lambda l, j: (l, 0, j)),
            pl.BlockSpec((1, 1, tn), lambda l, j: (l, 0, j)),
        ],
        out_specs=pl.BlockSpec((1, bsz, tn), lambda l, j: (l, 0, j)),
        out_shape=jax.ShapeDtypeStruct((nl, bsz, n), F32),
        scratch_shapes=[pltpu.VMEM((bsz, r), F32)],
        compiler_params=_params(("arbitrary", "arbitrary"), 32),
        name="adaln",
    )(c, w_down, w_up, b.reshape(nl, 1, n))


def _norm_mod_value(h_ref, g_ref, sc_ref, sh_ref):
    x = h_ref[...]
    ms = jnp.mean(x * x, axis=-1, keepdims=True)
    y = x * lax.rsqrt(ms + NORM_EPS) * g_ref[...]
    return y * (1.0 + sc_ref[0]) + sh_ref[0]


def _norm_kernel(h_ref, g_ref, sc_ref, sh_ref, o_ref):
    o_ref[...] = _norm_mod_value(h_ref, g_ref, sc_ref, sh_ref).astype(BF16)


def _norm_mod(h, gain, scale, shift, seq):
    m, d = h.shape
    bsz = scale.shape[0]
    bm = _tile(seq, 256)
    row_b = lambda i: ((i * bm) // seq, 0, 0)
    return pl.pallas_call(
        _norm_kernel,
        grid=(m // bm,),
        in_specs=[
            pl.BlockSpec((bm, d), lambda i: (i, 0)),
            pl.BlockSpec((1, d), lambda i: (0, 0)),
            pl.BlockSpec((1, 1, d), row_b),
            pl.BlockSpec((1, 1, d), row_b),
        ],
        out_specs=pl.BlockSpec((bm, d), lambda i: (i, 0)),
        out_shape=jax.ShapeDtypeStruct((m, d), BF16),
        compiler_params=_params(("arbitrary",), 40),
        name="norm_mod",
    )(h, gain.reshape(1, d), scale.reshape(bsz, 1, d), shift.reshape(bsz, 1, d))


def _norm_router_kernel(h_ref, g_ref, sc_ref, sh_ref, rw_ref, rb_ref,
                        np_ref, ri_ref, wt_ref, cnt_ref, carry_ref):
    @pl.when(pl.program_id(0) == 0)
    def _():
        carry_ref[...] = jnp.zeros_like(carry_ref)

    y = _norm_mod_value(h_ref, g_ref, sc_ref, sh_ref)
    half = y.shape[1] // 2
    np_ref[...] = _pack_bf16_pair(y[:, :half], y[:, half:])
    logits = jnp.dot(y.astype(BF16), rw_ref[...], preferred_element_type=F32) + rb_ref[...]
    bm = logits.shape[0]
    lane = lax.broadcasted_iota(I32, logits.shape, 1)

    l = logits
    sels, vals, idxs = [], [], []
    for _ in range(TOP_K):
        mx = jnp.max(l, axis=-1, keepdims=True)
        idx = jnp.min(jnp.where(l == mx, lane, LANES), axis=-1, keepdims=True)
        sel = lane == idx
        l = jnp.where(sel, -jnp.inf, l)
        sels.append(sel)
        vals.append(mx)
        idxs.append(idx)
    exps = [jnp.exp(v - vals[0]) for v in vals]
    denom = exps[0]
    for e in exps[1:]:
        denom = denom + e

    mask = sels[0].astype(F32)
    for sel in sels[1:]:
        mask = mask + sel.astype(F32)
    rr = lax.broadcasted_iota(I32, (bm, bm), 0)
    cc = lax.broadcasted_iota(I32, (bm, bm), 1)
    tri = (cc < rr).astype(BF16)
    carry = carry_ref[0:1, :]
    prefix = jnp.dot(tri, mask.astype(BF16), preferred_element_type=F32) + carry

    ri = jnp.zeros(logits.shape, I32)
    wt = jnp.zeros(logits.shape, F32)
    for k in range(TOP_K):
        rank = jnp.sum(jnp.where(sels[k], prefix, 0.0), axis=-1, keepdims=True).astype(I32)
        ri = jnp.where(lane == k, idxs[k], ri)
        ri = jnp.where(lane == TOP_K + k, rank, ri)
        wt = jnp.where(lane == k, exps[k] / denom, wt)
    ri_ref[...] = ri
    wt_ref[...] = wt
    new_carry = jnp.broadcast_to(carry + jnp.sum(mask, axis=0, keepdims=True), carry_ref.shape)
    carry_ref[...] = new_carry
    cnt_ref[...] = new_carry


def _norm_router(h, gain, scale, shift, rw, rb, layer, seq):
    m, d = h.shape
    bsz = scale.shape[0]
    bm = _tile(seq, 256)
    row_b = lambda i: ((i * bm) // seq, 0, 0)
    return pl.pallas_call(
        _norm_router_kernel,
        grid=(m // bm,),
        in_specs=[
            pl.BlockSpec((bm, d), lambda i: (i, 0)),
            pl.BlockSpec((1, d), lambda i: (0, 0)),
            pl.BlockSpec((1, 1, d), row_b),
            pl.BlockSpec((1, 1, d), row_b),
            pl.BlockSpec((None, d, LANES), lambda i: (layer, 0, 0)),
            pl.BlockSpec((None, 1, LANES), lambda i: (layer, 0, 0)),
        ],
        out_specs=[
            pl.BlockSpec((bm, d // 2), lambda i: (i, 0)),
            pl.BlockSpec((bm, LANES), lambda i: (i, 0)),
            pl.BlockSpec((bm, LANES), lambda i: (i, 0)),
            pl.BlockSpec((SUBLANES, LANES), lambda i: (0, 0)),
        ],
        out_shape=[
            jax.ShapeDtypeStruct((m, d // 2), U32),
            jax.ShapeDtypeStruct((m, LANES), I32),
            jax.ShapeDtypeStruct((m, LANES), F32),
            jax.ShapeDtypeStruct((SUBLANES, LANES), F32),
        ],
        scratch_shapes=[pltpu.VMEM((SUBLANES, LANES), F32)],
        compiler_params=_params(("arbitrary",), 40),
        name="norm_router",
    )(h, gain.reshape(1, d), scale.reshape(bsz, 1, d), shift.reshape(bsz, 1, d), rw, rb)


def _conv_in_kernel(n_ref, wb_ref, wc_ref, wv_ref, cw_ref, y_ref, tail_ref, *, seq):
    i = pl.program_id(0)
    j = pl.program_id(1)
    x = n_ref[...]
    bm = x.shape[0]
    b_gate = jnp.dot(x, wb_ref[...], preferred_element_type=F32)
    c_gate = jnp.dot(x, wc_ref[...], preferred_element_type=F32)
    v = jnp.dot(x, wv_ref[...], preferred_element_type=F32)
    u = c_gate * v
    prev = jnp.where((i * bm) % seq == 0, 0.0, tail_ref[j])
    tail_ref[j] = u[bm - SUBLANES:, :]
    u1 = pltpu.roll(u, 1, axis=0)
    u2 = pltpu.roll(u, 2, axis=0)
    row = lax.broadcasted_iota(I32, prev.shape, 0)
    head1 = jnp.where(row < 1, pltpu.roll(prev, 1, axis=0), u1[:SUBLANES])
    head2 = jnp.where(row < 2, pltpu.roll(prev, 2, axis=0), u2[:SUBLANES])
    u1 = jnp.concatenate([head1, u1[SUBLANES:]], axis=0)
    u2 = jnp.concatenate([head2, u2[SUBLANES:]], axis=0)
    cw = cw_ref[...]
    conv = cw[0:1, :] * u2 + cw[1:2, :] * u1 + cw[2:3, :] * u
    y_ref[...] = (b_gate * conv).astype(BF16)


def _conv_in(n, w_in, conv_w, layer, seq):
    m, d = n.shape
    bm = _tile(seq, 1024)
    bn = _tile(d, 256)
    nj = d // bn
    return pl.pallas_call(
        functools.partial(_conv_in_kernel, seq=seq),
        grid=(m // bm, nj),
        in_specs=[
            pl.BlockSpec((bm, d), lambda i, j: (i, 0)),
            pl.BlockSpec((None, d, bn), lambda i, j: (layer, 0, j)),
            pl.BlockSpec((None, d, bn), lambda i, j: (layer, 0, nj + j)),
            pl.BlockSpec((None, d, bn), lambda i, j: (layer, 0, 2 * nj + j)),
            pl.BlockSpec((None, CONV_W, bn), lambda i, j: (layer, 0, j)),
        ],
        out_specs=pl.BlockSpec((bm, bn), lambda i, j: (i, j)),
        out_shape=jax.ShapeDtypeStruct((m, d), BF16),
        scratch_shapes=[pltpu.VMEM((nj, SUBLANES, bn), F32)],
        compiler_params=_params(("arbitrary", "arbitrary"), 48),
        name="conv_in",
    )(n, w_in, w_in, w_in, conv_w)


def _matmul_res_kernel(a_ref, w_ref, h_ref, g_ref, o_ref):
    acc = jnp.dot(a_ref[...], w_ref[...], preferred_element_type=F32)
    o_ref[...] = h_ref[...] + g_ref[0] * acc


def _matmul_res(a, w, h, gate, layer, seq):
    m, k = a.shape
    n = w.shape[2]
    bsz = gate.shape[0]
    bm = _tile(seq, 1024)
    bn = _tile(n, 512)
    return pl.pallas_call(
        _matmul_res_kernel,
        grid=(m // bm, n // bn),
        in_specs=[
            pl.BlockSpec((bm, k), lambda i, j: (i, 0)),
            pl.BlockSpec((None, k, bn), lambda i, j: (layer, 0, j)),
            pl.BlockSpec((bm, bn), lambda i, j: (i, j)),
            pl.BlockSpec((1, 1, bn), lambda i, j: ((i * bm) // seq, 0, j)),
        ],
        out_specs=pl.BlockSpec((bm, bn), lambda i, j: (i, j)),
        out_shape=jax.ShapeDtypeStruct((m, n), F32),
        compiler_params=_params(("arbitrary", "arbitrary"), 48),
        name="matmul_res",
    )(a, w, h, gate.reshape(bsz, 1, n))


def _group_sumsq(t, bd_ref):
    tt = t * t
    hi = tt.astype(BF16)
    lo = (tt - hi.astype(F32)).astype(BF16)
    bd = bd_ref[...]
    return jnp.dot(hi, bd, preferred_element_type=F32) + jnp.dot(lo, bd, preferred_element_type=F32)


def _kv_kernel(n_ref, wk_ref, wvt_ref, gain_ref, bd_ref, k_ref, vt_ref, *, head_dim):
    x = n_ref[...]
    k = jnp.dot(x, wk_ref[...], preferred_element_type=F32)
    ssq = _group_sumsq(k, bd_ref)
    kn = k * lax.rsqrt(ssq * (1.0 / head_dim) + QK_EPS) * gain_ref[...]
    for g in range(k.shape[1] // head_dim):
        k_ref[g] = kn[:, g * head_dim:(g + 1) * head_dim].astype(BF16)
    vt_ref[...] = lax.dot_general(wvt_ref[...], x, (((1,), (1,)), ((), ())),
                                  preferred_element_type=F32).astype(BF16)


def _kv_proj(n, w_k, w_vt, k_gain, bd, head_dim, seq):
    m, d = n.shape
    kvd = w_k.shape[1]
    nkv = kvd // head_dim
    bm = _tile(seq, 512)
    return pl.pallas_call(
        functools.partial(_kv_kernel, head_dim=head_dim),
        grid=(m // bm,),
        in_specs=[
            pl.BlockSpec((bm, d), lambda i: (i, 0)),
            pl.BlockSpec((d, kvd), lambda i: (0, 0)),
            pl.BlockSpec((kvd, d), lambda i: (0, 0)),
            pl.BlockSpec((1, kvd), lambda i: (0, 0)),
            pl.BlockSpec((kvd, kvd), lambda i: (0, 0)),
        ],
        out_specs=[pl.BlockSpec((nkv, bm, head_dim), lambda i: (0, i, 0)),
                   pl.BlockSpec((kvd, bm), lambda i: (0, i))],
        out_shape=[jax.ShapeDtypeStruct((nkv, m, head_dim), BF16),
                   jax.ShapeDtypeStruct((kvd, m), BF16)],
        compiler_params=_params(("arbitrary",), 48),
        name="kv_proj",
    )(n, w_k, w_vt, k_gain, bd)


def _q_attn_kernel(n_ref, wqt_ref, gain_ref, kc_ref, kp_ref, vc_ref, vp_ref, bias_ref, sink_ref,
                   o_ref, *, seq, head_dim):
    i = pl.program_id(0)
    bm = n_ref.shape[0]
    group = wqt_ref.shape[0] // head_dim
    qt = lax.dot_general(wqt_ref[...], n_ref[...], (((1,), (1,)), ((), ())), preferred_element_type=F32)
    gain = gain_ref[...]
    qn = []
    for hh in range(group):
        qh = qt[hh * head_dim:(hh + 1) * head_dim, :]
        rs = lax.rsqrt(jnp.sum(qh * qh, axis=0, keepdims=True) * (1.0 / head_dim) + QK_EPS)
        qn.append((qh * rs * gain[hh * head_dim:(hh + 1) * head_dim, :]).astype(BF16))
    sink = sink_ref[...]
    shape = (WINDOW, group * WINDOW)
    key_c = lax.broadcasted_iota(I32, shape, 0)
    query_i = lax.broadcasted_iota(I32, shape, 1) & (WINDOW - 1)
    in_cur = key_c <= query_i
    for qb in range(bm // WINDOW):
        r0 = qb * WINDOW
        q_all = jnp.concatenate([q[:, r0:r0 + WINDOW] for q in qn], axis=1)
        if qb == 0:
            k_prev, vt_prev = kp_ref[0], vp_ref[...]
        else:
            k_prev, vt_prev = kc_ref[0, r0 - WINDOW:r0, :], vc_ref[:, r0 - WINDOW:r0]
        k2 = jnp.concatenate([k_prev, kc_ref[0, r0:r0 + WINDOW, :]], axis=0)
        v2t = jnp.concatenate([vt_prev, vc_ref[:, r0:r0 + WINDOW]], axis=1)
        st = jnp.dot(k2, q_all, preferred_element_type=F32)
        first = ((i * bm + r0) % seq == 0).astype(I32)
        s = jnp.where(in_cur, st[WINDOW:], st[:WINDOW]) + bias_ref[first]
        mx = jnp.maximum(jnp.max(s, axis=0, keepdims=True), sink)
        p = jnp.exp(s - mx)
        denom = jnp.sum(p, axis=0, keepdims=True) + jnp.exp(sink - mx)
        pt = jnp.concatenate([jnp.where(in_cur, 0.0, p), jnp.where(in_cur, p, 0.0)], axis=0).astype(BF16)
        ot = jnp.dot(v2t, pt, preferred_element_type=F32) * (1.0 / denom)
        for pr in range(group // 2):
            blk = jnp.concatenate([ot[:, (2 * pr) * WINDOW:(2 * pr + 1) * WINDOW],
                                   ot[:, (2 * pr + 1) * WINDOW:(2 * pr + 2) * WINDOW]], axis=0)
            o_ref[r0:r0 + WINDOW, pr * 2 * head_dim:(pr + 1) * 2 * head_dim] = blk.T.astype(BF16)


def _q_attn(n, w_qt, q_gain, k, vt, bias, sink, layer, head_dim, seq):
    m, d = n.shape
    nkv = k.shape[0]
    gw = w_qt.shape[1] // nkv
    group = gw // head_dim
    assert head_dim == WINDOW // 2 and group % 2 == 0
    bm = q_gain.shape[1]
    nb = bm // WINDOW
    return pl.pallas_call(
        functools.partial(_q_attn_kernel, seq=seq, head_dim=head_dim),
        grid=(m // bm, nkv),
        in_specs=[
            pl.BlockSpec((bm, d), lambda i, g: (i, 0)),
            pl.BlockSpec((None, gw, d), lambda i, g: (layer, g, 0)),
            pl.BlockSpec((gw, bm), lambda i, g: (0, 0)),
            pl.BlockSpec((1, bm, head_dim), lambda i, g: (g, i, 0)),
            pl.BlockSpec((1, WINDOW, head_dim), lambda i, g: (g, jnp.maximum(i * nb - 1, 0), 0)),
            pl.BlockSpec((head_dim, bm), lambda i, g: (g, i)),
            pl.BlockSpec((head_dim, WINDOW), lambda i, g: (g, jnp.maximum(i * nb - 1, 0))),
            pl.BlockSpec((None, 2, WINDOW, group * WINDOW), lambda i, g: (g, 0, 0, 0)),
            pl.BlockSpec((None, 1, group * WINDOW), lambda i, g: (g, 0, 0)),
        ],
        out_specs=pl.BlockSpec((bm, gw), lambda i, g: (i, g)),
        out_shape=jax.ShapeDtypeStruct((m, w_qt.shape[1]), BF16),
        compiler_params=_params(("arbitrary", "arbitrary"), 48),
        name="q_attn",
    )(n, w_qt, q_gain, k, k, vt, vt, bias, sink)


def _attn_bias(n_heads, nkv):
    group = n_heads // nkv
    h = np.arange(1, n_heads + 1, dtype=np.float32)
    slopes = np.power(2.0, -8.0 * h / n_heads).astype(np.float32)
    c = np.arange(WINDOW)[:, None]
    qi = np.arange(WINDOW)[None, :]
    in_cur = c <= qi
    dist = np.where(in_cur, qi - c, WINDOW + qi - c).astype(np.float32)
    alibi = -slopes[:, None, None] * dist[None]
    first = np.where(in_cur[None], alibi, -np.inf).astype(np.float32)
    tables = np.stack([alibi, first], axis=1)
    tables = tables.reshape(nkv, group, 2, WINDOW, WINDOW).transpose(0, 2, 3, 1, 4)
    return jnp.asarray(tables.reshape(nkv, 2, WINDOW, group * WINDOW))


def _expert_ffn(x_ref, y_ref, wgu_ref, bgu_ref, wd_ref, bd_ref):
    lo, hi = _unpack_bf16_pair(x_ref[...])
    half = lo.shape[1]
    f = wd_ref.shape[1]
    x = jnp.concatenate([lo.astype(BF16), hi.astype(BF16)], axis=1)
    kc = x.shape[1] // MOE_IN_CHUNKS
    f_split = [(0, min(f, MXU_DEPTH))] + ([(MXU_DEPTH, f)] if f > MXU_DEPTH else [])
    hids = []
    for f0, f1 in f_split:
        bias = jnp.concatenate([bgu_ref[0, :, f0:f1], bgu_ref[0, :, f + f0:f + f1]], axis=1)
        row_parts = 2 if bias.shape[1] <= MXU_DEPTH else 1
        rp = x.shape[0] // row_parts
        parts = [bias] * row_parts
        for ch in range(MOE_IN_CHUNKS):
            k0 = ch * kc
            w = jnp.concatenate([wgu_ref[0, k0:k0 + kc, f0:f1], wgu_ref[0, k0:k0 + kc, f + f0:f + f1]],
                                axis=1).astype(BF16)
            parts = [p + jnp.dot(x[i * rp:(i + 1) * rp, k0:k0 + kc], w, preferred_element_type=F32)
                     for i, p in enumerate(parts)]
        gl = parts[0] if row_parts == 1 else jnp.concatenate(parts, axis=0)
        glu = jnp.minimum(gl[:, :f1 - f0], SWIGLU_LIMIT)
        lin = jnp.clip(gl[:, f1 - f0:], -SWIGLU_LIMIT, SWIGLU_LIMIT)
        hids.append((glu * jax.nn.sigmoid(SWIGLU_ALPHA * glu) * (lin + 1.0)).astype(BF16))
    cw = half // MOE_OUT_CHUNKS
    for ch in range(MOE_OUT_CHUNKS):
        a0, b0 = ch * cw, half + ch * cw
        ya = bd_ref[0, :, a0:a0 + cw]
        yb = bd_ref[0, :, b0:b0 + cw]
        for (f0, f1), hid in zip(f_split, hids):
            ya = ya + jnp.dot(hid, wd_ref[0, f0:f1, a0:a0 + cw].astype(BF16), preferred_element_type=F32)
            yb = yb + jnp.dot(hid, wd_ref[0, f0:f1, b0:b0 + cw].astype(BF16), preferred_element_type=F32)
        y_ref[:, a0:a0 + cw] = _pack_bf16_pair(ya, yb)


def _dispatch_kernel(pos_ref, n_ref, xs_in, xs_out, sem, *, bt):
    for r in range(bt):
        for k in range(TOP_K):
            p = pos_ref[0, 0, r * TOP_K + k]
            pltpu.make_async_copy(n_ref.at[pl.ds(r, 1)], xs_out.at[pl.ds(p, 1)], sem).start()
    pltpu.make_async_copy(xs_in.at[pl.ds(0, bt * TOP_K)], xs_out.at[pl.ds(0, bt * TOP_K)], sem).wait()


def _dispatch(pos, n_packed, xs):
    m, dh = n_packed.shape
    bt = _tile(m, 512)
    return pl.pallas_call(
        functools.partial(_dispatch_kernel, bt=bt),
        grid=(m // bt,),
        in_specs=[
            pl.BlockSpec((1, 1, bt * TOP_K), lambda i: (i, 0, 0), memory_space=pltpu.SMEM),
            pl.BlockSpec((bt, dh), lambda i: (i, 0)),
            pl.BlockSpec(memory_space=pl.ANY),
        ],
        out_specs=pl.BlockSpec(memory_space=pl.ANY),
        out_shape=jax.ShapeDtypeStruct(xs.shape, xs.dtype),
        scratch_shapes=[pltpu.SemaphoreType.DMA(())],
        input_output_aliases={2: 0},
        compiler_params=_params(("arbitrary",), 24),
        name="moe_dispatch",
    )(pos.reshape(m // bt, 1, bt * TOP_K), n_packed, xs)


def _moe_kernel(te_ref, nv_ref, xs_ref, wgu_ref, bgu_ref, wd_ref, bd_ref, ys_ref):
    del te_ref
    valid = pl.program_id(0) < nv_ref[0]

    @pl.when(jnp.logical_not(valid))
    def _():
        ys_ref[...] = jnp.zeros_like(ys_ref)

    @pl.when(valid)
    def _():
        _expert_ffn(xs_ref, ys_ref, wgu_ref, bgu_ref, wd_ref, bd_ref)


def _moe_grouped(tile_expert, n_valid, xs, w_gu, b_gu, w_d, b_d, layer, tm):
    rows, dh = xs.shape
    _, ne, d, f2 = w_gu.shape
    f = f2 // 2
    assert dh % MOE_OUT_CHUNKS == 0
    by_expert = lambda j, te, nv: (layer, te[j], 0, 0)
    grid_spec = pltpu.PrefetchScalarGridSpec(
        num_scalar_prefetch=2,
        grid=(rows // tm,),
        in_specs=[
            pl.BlockSpec((tm, dh), lambda j, te, nv: (j, 0)),
            pl.BlockSpec((None, 1, d, f2), by_expert),
            pl.BlockSpec((None, 1, 1, f2), by_expert),
            pl.BlockSpec((None, 1, f, d), by_expert),
            pl.BlockSpec((None, 1, 1, d), by_expert),
        ],
        out_specs=pl.BlockSpec((tm, dh), lambda j, te, nv: (j, 0)),
    )
    return pl.pallas_call(
        _moe_kernel,
        grid_spec=grid_spec,
        out_shape=jax.ShapeDtypeStruct((rows, dh), U32),
        compiler_params=_params(("arbitrary",), 56),
        name="moe_grouped",
    )(tile_expert, n_valid, xs, w_gu, b_gu, w_d, b_d)


COMBINE_ROWS = 16
COMBINE_WORDS = 256


def _combine_kernel(*refs, n_norms):
    pos_ref, pos_next_ref, h_ref, wt_ref, g_ref = refs[:5]
    norm_refs = refs[5:5 + 2 * n_norms]
    ys_hbm = refs[5 + 2 * n_norms]
    o_ref = refs[6 + 2 * n_norms]
    n_refs = refs[7 + 2 * n_norms:7 + 3 * n_norms]
    buf0, buf1, sem = refs[7 + 3 * n_norms:]
    bc, d = h_ref.shape
    dh = d // 2
    rows, cw = COMBINE_ROWS, COMBINE_WORDS
    i = pl.program_id(0)
    last = pl.num_programs(0) - 1

    def gather_rows(p_ref, buf, s):
        for r in range(bc):
            for k in range(TOP_K):
                pltpu.make_async_copy(ys_hbm.at[pl.ds(p_ref[0, 0, r * TOP_K + k], 1)],
                                      buf.at[k, pl.ds(r, 1)], s).start()

    @pl.when(i == 0)
    def _():
        gather_rows(pos_ref, buf0, sem.at[0])

    def step(par):
        cur, nxt = (buf0, buf1) if par == 0 else (buf1, buf0)

        @pl.when(i < last)
        def _():
            gather_rows(pos_next_ref, nxt, sem.at[1 - par])

        for k in range(TOP_K):
            pltpu.make_async_copy(ys_hbm.at[pl.ds(0, bc)], cur.at[k], sem.at[par]).wait()
        lax.fori_loop(0, bc // rows, functools.partial(combine_rows, cur), 0)

    def combine_rows(y_ref, r, carry):
        rs = pl.ds(pl.multiple_of(r * rows, rows), rows)
        wt = wt_ref[rs, :]
        wk = [jnp.broadcast_to(wt[:, k:k + 1], (rows, cw)) for k in range(TOP_K)]
        ssq = jnp.zeros((rows, LANES), F32)
        for c in range(dh // cw):
            acc_lo = acc_hi = None
            for k in range(TOP_K):
                lo, hi = _unpack_bf16_pair(y_ref[k, rs, c * cw:(c + 1) * cw])
                acc_lo = wk[k] * lo if acc_lo is None else acc_lo + wk[k] * lo
                acc_hi = wk[k] * hi if acc_hi is None else acc_hi + wk[k] * hi
            a0, b0 = c * cw, dh + c * cw
            ha = h_ref[rs, a0:a0 + cw] + g_ref[0, :, a0:a0 + cw] * acc_lo
            hb = h_ref[rs, b0:b0 + cw] + g_ref[0, :, b0:b0 + cw] * acc_hi
            o_ref[rs, a0:a0 + cw] = ha
            o_ref[rs, b0:b0 + cw] = hb
            if n_norms:
                for t in range(cw // LANES):
                    va = ha[:, t * LANES:(t + 1) * LANES]
                    vb = hb[:, t * LANES:(t + 1) * LANES]
                    ssq = ssq + va * va + vb * vb
        if n_norms:
            rsq = lax.rsqrt(jnp.sum(ssq, axis=-1, keepdims=True) * (1.0 / d) + NORM_EPS)
            for c in range(d // (2 * cw)):
                cs = slice(c * 2 * cw, (c + 1) * 2 * cw)
                xn = o_ref[rs, cs] * rsq
                for t in range(n_norms):
                    gs_ref, sh_ref = norm_refs[2 * t], norm_refs[2 * t + 1]
                    n_refs[t][rs, cs] = (xn * gs_ref[0, :, cs] + sh_ref[0, :, cs]).astype(BF16)
        return carry

    @pl.when(i % 2 == 0)
    def _():
        step(0)

    @pl.when(i % 2 == 1)
    def _():
        step(1)


def _combine(pos, ys, h, wt, gate, norms, seq):
    m, d = h.shape
    bsz = gate.shape[0]
    bc = _tile(seq, 256)
    nblk = m // bc
    pos_blocks = pos.reshape(nblk, 1, bc * TOP_K)
    pos_tile = lambda shift: pl.BlockSpec(
        (1, 1, bc * TOP_K), lambda i: (jnp.minimum(i + shift, nblk - 1), 0, 0), memory_space=pltpu.SMEM)
    per_batch = pl.BlockSpec((1, 1, d), lambda i: ((i * bc) // seq, 0, 0))
    row_tile = pl.BlockSpec((bc, d), lambda i: (i, 0))
    norm_args = [v.reshape(bsz, 1, d) for pair in norms for v in pair]
    outs = pl.pallas_call(
        functools.partial(_combine_kernel, n_norms=len(norms)),
        grid=(nblk,),
        in_specs=[
            pos_tile(0),
            pos_tile(1),
            row_tile,
            pl.BlockSpec((bc, LANES), lambda i: (i, 0)),
            per_batch,
        ] + [per_batch] * len(norm_args) + [pl.BlockSpec(memory_space=pl.ANY)],
        out_specs=[row_tile] * (1 + len(norms)),
        out_shape=[jax.ShapeDtypeStruct((m, d), F32)] + [jax.ShapeDtypeStruct((m, d), BF16)] * len(norms),
        scratch_shapes=[pltpu.VMEM((TOP_K, bc, d // 2), U32)] * 2 + [pltpu.SemaphoreType.DMA((2,))],
        compiler_params=_params(("arbitrary",), 56),
        name="moe_combine",
    )(pos_blocks, pos_blocks, h, wt, gate.reshape(bsz, 1, d), *norm_args, ys)
    return outs[0], list(outs[1:])


def _moe_layer(h, gain, scale, shift, gate, rw, rb, w_gu, b_gu, w_d, b_d, xs, next_norms, layer, seq, tm):
    m, d = h.shape
    ne = w_gu.shape[1]
    n_packed, route, wt, counts = _norm_router(h, gain, scale, shift, rw, rb, layer, seq)

    cnt = counts[0, :ne].astype(I32)
    padded = ((cnt + tm - 1) // tm) * tm
    ends = jnp.cumsum(padded)
    starts = ends - padded
    expert_ids = jnp.arange(ne, dtype=I32)
    idx = route[:, :TOP_K]
    pos = jnp.sum(jnp.where(idx[:, :, None] == expert_ids, starts, 0), axis=-1) + route[:, TOP_K:2 * TOP_K]
    n_tiles = xs.shape[0] // tm
    n_valid = ends[-1] // tm
    tile_id = jnp.arange(n_tiles, dtype=I32)
    tile_start = jnp.minimum(tile_id, n_valid - 1) * tm
    tile_expert = jnp.sum((ends[None, :] <= tile_start[:, None]).astype(I32), axis=1)

    xs = _dispatch(pos, n_packed, xs)
    ys = _moe_grouped(tile_expert, n_valid.reshape(1).astype(I32), xs, w_gu, b_gu, w_d, b_d, layer, tm)
    h, normed = _combine(pos, ys, h, wt, gate, next_norms, seq)
    return h, normed, xs


def kernel(x, c, norm_mix, norm_ffn, mod_down, mod_up, mod_b, a_w_in, a_conv, a_w_out, kv_norm, kv_mod_down, kv_mod_up, kv_mod_b, w_kv, k_norm, b_w_q, b_q_norm, b_sinks, b_w_o, router_w, router_b, moe_w_gu, moe_b_gu, moe_w_down, moe_b_down):
    bsz, seq, d = x.shape
    m = bsz * seq
    depth = norm_mix.shape[0]
    n_a = a_w_in.shape[0]
    head_dim = k_norm.shape[0]
    n_heads = b_sinks.shape[1]
    nkv = w_kv.shape[1] // (2 * head_dim)
    group = n_heads // nkv
    ne = router_w.shape[2]
    tm = _tile(m * TOP_K, 256)
    attn_bm = _tile(seq, 512)
    assert seq % WINDOW == 0 and n_a >= 1

    mods = _adaln(c, mod_down, mod_up, mod_b)
    kv_mods = _adaln(c, kv_mod_down[None], kv_mod_up[None], kv_mod_b[None])[0]

    w_in = a_w_in.astype(BF16)
    w_out = a_w_out.astype(BF16)
    w_qt = jnp.swapaxes(b_w_q, 1, 2).astype(BF16)
    w_o = b_w_o.astype(BF16)
    kvd = nkv * head_dim
    w_k = w_kv[:, :kvd].astype(BF16)
    w_vt = w_kv[:, kvd:].T.astype(BF16)
    w_gu = moe_w_gu
    w_d = moe_w_down
    b_gu = moe_b_gu[:, :, None, :]
    b_d = moe_b_down[:, :, None, :]
    rw = jnp.zeros((depth, d, LANES), BF16).at[:, :, :ne].set(router_w.astype(BF16))
    rb = jnp.full((depth, 1, LANES), NEG_BIG, F32).at[:, 0, :ne].set(router_b)

    gw = group * head_dim
    bd_kv = jnp.asarray(np.kron(np.eye(nkv, dtype=np.float32),
                                np.ones((head_dim, head_dim), np.float32)), BF16)
    attn_bias = _attn_bias(n_heads, nkv)

    h = x.reshape(m, d)
    xs = jnp.zeros((m * TOP_K + ne * tm, d // 2), U32)
    sh_kv, sc_kv = jnp.split(kv_mods, 2, axis=-1)
    k = vt = None
    sh_m, sc_m = mods[0, :, :d], mods[0, :, d:2 * d]
    n = _norm_mod(h, norm_mix[0], sc_m, sh_m, seq)
    n_kv = None
    for layer in range(depth):
        _, _, g_m, sh_f, sc_f, g_f = jnp.split(mods[layer], 6, axis=-1)
        if layer < n_a:
            y = _conv_in(n, w_in, a_conv, layer, seq)
            h = _matmul_res(y, w_out, h, g_m, layer, seq)
        else:
            if layer == n_a:
                k, vt = _kv_proj(n_kv, w_k, w_vt, jnp.tile(k_norm, nkv).reshape(1, kvd), bd_kv, head_dim, seq)
            i = layer - n_a
            q_gain = jnp.broadcast_to((jnp.tile(b_q_norm[i], group) * head_dim ** -0.5)[:, None], (gw, attn_bm))
            sink = jnp.repeat(b_sinks[i].reshape(nkv, group), WINDOW, axis=1).reshape(nkv, 1, group * WINDOW)
            a = _q_attn(n, w_qt, q_gain, k, vt, attn_bias, sink, i, head_dim, seq)
            h = _matmul_res(a, w_o, h, g_m, i, seq)
        next_norms = []
        if layer + 1 < depth:
            sh_n, sc_n = mods[layer + 1, :, :d], mods[layer + 1, :, d:2 * d]
            next_norms.append((norm_mix[layer + 1] * (1.0 + sc_n), sh_n))
            if layer + 1 == n_a:
                next_norms.append((kv_norm * (1.0 + sc_kv), sh_kv))
        h, normed, xs = _moe_layer(h, norm_ffn[layer], sc_f, sh_f, g_f, rw, rb, w_gu, b_gu, w_d, b_d, xs,
                                   next_norms, layer, seq, tm)
        if normed:
            n = normed[0]
            n_kv = normed[1] if len(normed) > 1 else None
    return h.reshape(bsz, seq, d)
```

```python
import functools

import numpy as np
import jax
import jax.numpy as jnp
from jax import lax
from jax.experimental import pallas as pl
from jax.experimental.pallas import tpu as pltpu

WINDOW = 128
TOP_K = 4
CONV_W = 3
NORM_EPS = 1e-5
QK_EPS = 1e-6
SWIGLU_ALPHA = 1.702
SWIGLU_LIMIT = 7.0
LANES = 128
SUBLANES = 8
MXU_DEPTH = 256
NEG_BIG = -1e30
MOE_OUT_CHUNKS = 2
MOE_IN_CHUNKS = 4
MIB = 1024 * 1024

F32 = jnp.float32
BF16 = jnp.bfloat16
U32 = jnp.uint32
I32 = jnp.int32
HI_MASK = np.uint32(0xFFFF0000)


def _tile(total, pref):
    t = min(total, pref)
    assert total % t == 0, (total, pref)
    return t


def _params(sem, vmem_mib):
    return pltpu.CompilerParams(dimension_semantics=sem, vmem_limit_bytes=vmem_mib * MIB)


def _pack_bf16_pair(a, b):
    lo = lax.bitcast_convert_type(a.astype(BF16).astype(F32), U32) >> 16
    hi = lax.bitcast_convert_type(b.astype(BF16).astype(F32), U32) & HI_MASK
    return lo | hi


def _unpack_bf16_pair(w):
    lo = lax.bitcast_convert_type(w << 16, F32)
    hi = lax.bitcast_convert_type(w & HI_MASK, F32)
    return lo, hi


def _adaln_kernel(c_ref, wd_ref, wu_ref, b_ref, o_ref, d_ref):
    @pl.when(pl.program_id(1) == 0)
    def _():
        c = c_ref[...]
        s = c * jax.nn.sigmoid(c)
        d_ref[...] = jnp.dot(s, wd_ref[0], preferred_element_type=F32, precision=lax.Precision.HIGHEST)

    o_ref[0] = jnp.dot(d_ref[...], wu_ref[0], preferred_element_type=F32,
                       precision=lax.Precision.HIGHEST) + b_ref[0]


def _adaln(c, w_down, w_up, b):
    nl, d, r = w_down.shape
    n = w_up.shape[2]
    bsz = c.shape[0]
    tn = _tile(n, min(d, 2048))
    return pl.pallas_call(
        _adaln_kernel,
        grid=(nl, n // tn),
        in_specs=[
            pl.BlockSpec((bsz, d), lambda l, j: (0, 0)),
            pl.BlockSpec((1, d, r), lambda l, j: (l, 0, 0)),
            pl.BlockSpec((1, r, tn), lambda l, j: (l, 0, j)),
            pl.BlockSpec((1, 1, tn), lambda l, j: (l, 0, j)),
        ],
        out_specs=pl.BlockSpec((1, bsz, tn), lambda l, j: (l, 0, j)),
        out_shape=jax.ShapeDtypeStruct((nl, bsz, n), F32),
        scratch_shapes=[pltpu.VMEM((bsz, r), F32)],
        compiler_params=_params(("arbitrary", "arbitrary"), 32),
        name="adaln",
    )(c, w_down, w_up, b.reshape(nl, 1, n))


def _norm_mod_value(h_ref, g_ref, sc_ref, sh_ref):
    x = h_ref[...]
    ms = jnp.mean(x * x, axis=-1, keepdims=True)
    y = x * lax.rsqrt(ms + NORM_EPS) * g_ref[...]
    return y * (1.0 + sc_ref[0]) + sh_ref[0]


def _norm_kernel(h_ref, g_ref, sc_ref, sh_ref, o_ref):
    o_ref[...] = _norm_mod_value(h_ref, g_ref, sc_ref, sh_ref).astype(BF16)


def _norm_mod(h, gain, scale, shift, seq):
    m, d = h.shape
    bsz = scale.shape[0]
    bm = _tile(seq, 256)
    row_b = lambda i: ((i * bm) // seq, 0, 0)
    return pl.pallas_call(
        _norm_kernel,
        grid=(m // bm,),
        in_specs=[
            pl.BlockSpec((bm, d), lambda i: (i, 0)),
            pl.BlockSpec((1, d), lambda i: (0, 0)),
            pl.BlockSpec((1, 1, d), row_b),
            pl.BlockSpec((1, 1, d), row_b),
        ],
        out_specs=pl.BlockSpec((bm, d), lambda i: (i, 0)),
        out_shape=jax.ShapeDtypeStruct((m, d), BF16),
        compiler_params=_params(("arbitrary",), 40),
        name="norm_mod",
    )(h, gain.reshape(1, d), scale.reshape(bsz, 1, d), shift.reshape(bsz, 1, d))


def _norm_router_kernel(h_ref, g_ref, sc_ref, sh_ref, rw_ref, rb_ref,
                        np_ref, ri_ref, wt_ref, cnt_ref, carry_ref):
    @pl.when(pl.program_id(0) == 0)
    def _():
        carry_ref[...] = jnp.zeros_like(carry_ref)

    y = _norm_mod_value(h_ref, g_ref, sc_ref, sh_ref)
    half = y.shape[1] // 2
    np_ref[...] = _pack_bf16_pair(y[:, :half], y[:, half:])
    logits = jnp.dot(y.astype(BF16), rw_ref[...], preferred_element_type=F32) + rb_ref[...]
    bm = logits.shape[0]
    lane = lax.broadcasted_iota(I32, logits.shape, 1)

    l = logits
    sels, vals, idxs = [], [], []
    for _ in range(TOP_K):
        mx = jnp.max(l, axis=-1, keepdims=True)
        idx = jnp.min(jnp.where(l == mx, lane, LANES), axis=-1, keepdims=True)
        sel = lane == idx
        l = jnp.where(sel, -jnp.inf, l)
        sels.append(sel)
        vals.append(mx)
        idxs.append(idx)
    exps = [jnp.exp(v - vals[0]) for v in vals]
    denom = exps[0]
    for e in exps[1:]:
        denom = denom + e

    mask = sels[0].astype(F32)
    for sel in sels[1:]:
        mask = mask + sel.astype(F32)
    rr = lax.broadcasted_iota(I32, (bm, bm), 0)
    cc = lax.broadcasted_iota(I32, (bm, bm), 1)
    tri = (cc < rr).astype(BF16)
    carry = carry_ref[0:1, :]
    prefix = jnp.dot(tri, mask.astype(BF16), preferred_element_type=F32) + carry

    ri = jnp.zeros(logits.shape, I32)
    wt = jnp.zeros(logits.shape, F32)
    for k in range(TOP_K):
        rank = jnp.sum(jnp.where(sels[k], prefix, 0.0), axis=-1, keepdims=True).astype(I32)
        ri = jnp.where(lane == k, idxs[k], ri)
        ri = jnp.where(lane == TOP_K + k, rank, ri)
        wt = jnp.where(lane == k, exps[k] / denom, wt)
    ri_ref[...] = ri
    wt_ref[...] = wt
    new_carry = jnp.broadcast_to(carry + jnp.sum(mask, axis=0, keepdims=True), carry_ref.shape)
    carry_ref[...] = new_carry
    cnt_ref[...] = new_carry


def _norm_router(h, gain, scale, shift, rw, rb, layer, seq):
    m, d = h.shape
    bsz = scale.shape[0]
    bm = _tile(seq, 256)
    row_b = lambda i: ((i * bm) // seq, 0, 0)
    return pl.pallas_call(
        _norm_router_kernel,
        grid=(m // bm,),
        in_specs=[
            pl.BlockSpec((bm, d), lambda i: (i, 0)),
            pl.BlockSpec((1, d), lambda i: (0, 0)),
            pl.BlockSpec((1, 1, d), row_b),
            pl.BlockSpec((1, 1, d), row_b),
            pl.BlockSpec((None, d, LANES), lambda i: (layer, 0, 0)),
            pl.BlockSpec((None, 1, LANES), lambda i: (layer, 0, 0)),
        ],
        out_specs=[
            pl.BlockSpec((bm, d // 2), lambda i: (i, 0)),
            pl.BlockSpec((bm, LANES), lambda i: (i, 0)),
            pl.BlockSpec((bm, LANES), lambda i: (i, 0)),
            pl.BlockSpec((SUBLANES, LANES), lambda i: (0, 0)),
        ],
        out_shape=[
            jax.ShapeDtypeStruct((m, d // 2), U32),
            jax.ShapeDtypeStruct((m, LANES), I32),
            jax.ShapeDtypeStruct((m, LANES), F32),
            jax.ShapeDtypeStruct((SUBLANES, LANES), F32),
        ],
        scratch_shapes=[pltpu.VMEM((SUBLANES, LANES), F32)],
        compiler_params=_params(("arbitrary",), 40),
        name="norm_router",
    )(h, gain.reshape(1, d), scale.reshape(bsz, 1, d), shift.reshape(bsz, 1, d), rw, rb)


def _conv_in_kernel(n_ref, wb_ref, wc_ref, wv_ref, cw_ref, y_ref, tail_ref, *, seq):
    i = pl.program_id(0)
    j = pl.program_id(1)
    x = n_ref[...]
    bm = x.shape[0]
    b_gate = jnp.dot(x, wb_ref[...], preferred_element_type=F32)
    c_gate = jnp.dot(x, wc_ref[...], preferred_element_type=F32)
    v = jnp.dot(x, wv_ref[...], preferred_element_type=F32)
    u = c_gate * v
    prev = jnp.where((i * bm) % seq == 0, 0.0, tail_ref[j])
    tail_ref[j] = u[bm - SUBLANES:, :]
    u1 = pltpu.roll(u, 1, axis=0)
    u2 = pltpu.roll(u, 2, axis=0)
    row = lax.broadcasted_iota(I32, prev.shape, 0)
    head1 = jnp.where(row < 1, pltpu.roll(prev, 1, axis=0), u1[:SUBLANES])
    head2 = jnp.where(row < 2, pltpu.roll(prev, 2, axis=0), u2[:SUBLANES])
    u1 = jnp.concatenate([head1, u1[SUBLANES:]], axis=0)
    u2 = jnp.concatenate([head2, u2[SUBLANES:]], axis=0)
    cw = cw_ref[...]
    conv = cw[0:1, :] * u2 + cw[1:2, :] * u1 + cw[2:3, :] * u
    y_ref[...] = (b_gate * conv).astype(BF16)


def _conv_in(n, w_in, conv_w, layer, seq):
    m, d = n.shape
    bm = _tile(seq, 1024)
    bn = _tile(d, 256)
    nj = d // bn
    return pl.pallas_call(
        functools.partial(_conv_in_kernel, seq=seq),
        grid=(m // bm, nj),
        in_specs=[
            pl.BlockSpec((bm, d), lambda i, j: (i, 0)),
            pl.BlockSpec((None, d, bn), lambda i, j: (layer, 0, j)),
            pl.BlockSpec((None, d, bn), lambda i, j: (layer, 0, nj + j)),
            pl.BlockSpec((None, d, bn), lambda i, j: (layer, 0, 2 * nj + j)),
            pl.BlockSpec((None, CONV_W, bn), lambda i, j: (layer, 0, j)),
        ],
        out_specs=pl.BlockSpec((bm, bn), lambda i, j: (i, j)),
        out_shape=jax.ShapeDtypeStruct((m, d), BF16),
        scratch_shapes=[pltpu.VMEM((nj, SUBLANES, bn), F32)],
        compiler_params=_params(("arbitrary", "arbitrary"), 48),
        name="conv_in",
    )(n, w_in, w_in, w_in, conv_w)


def _matmul_res_kernel(a_ref, w_ref, h_ref, g_ref, o_ref):
    acc = jnp.dot(a_ref[...], w_ref[...], preferred_element_type=F32)
    o_ref[...] = h_ref[...] + g_ref[0] * acc


def _matmul_res(a, w, h, gate, layer, seq):
    m, k = a.shape
    n = w.shape[2]
    bsz = gate.shape[0]
    bm = _tile(seq, 1024)
    bn = _tile(n, 512)
    return pl.pallas_call(
        _matmul_res_kernel,
        grid=(m // bm, n // bn),
        in_specs=[
            pl.BlockSpec((bm, k), lambda i, j: (i, 0)),
            pl.BlockSpec((None, k, bn), lambda i, j: (layer, 0, j)),
            pl.BlockSpec((bm, bn), lambda i, j: (i, j)),
            pl.BlockSpec((1, 1, bn), lambda i, j: ((i * bm) // seq, 0, j)),
        ],
        out_specs=pl.BlockSpec((bm, bn), lambda i, j: (i, j)),
        out_shape=jax.ShapeDtypeStruct((m, n), F32),
        compiler_params=_params(("arbitrary", "arbitrary"), 48),
        name="matmul_res",
    )(a, w, h, gate.reshape(bsz, 1, n))


def _group_sumsq(t, bd_ref):
    tt = t * t
    hi = tt.astype(BF16)
    lo = (tt - hi.astype(F32)).astype(BF16)
    bd = bd_ref[...]
    return jnp.dot(hi, bd, preferred_element_type=F32) + jnp.dot(lo, bd, preferred_element_type=F32)


def _kv_kernel(n_ref, wk_ref, wvt_ref, gain_ref, bd_ref, k_ref, vt_ref, *, head_dim):
    x = n_ref[...]
    k = jnp.dot(x, wk_ref[...], preferred_element_type=F32)
    ssq = _group_sumsq(k, bd_ref)
    kn = k * lax.rsqrt(ssq * (1.0 / head_dim) + QK_EPS) * gain_ref[...]
    for g in range(k.shape[1] // head_dim):
        k_ref[g] = kn[:, g * head_dim:(g + 1) * head_dim].astype(BF16)
    vt_ref[...] = lax.dot_general(wvt_ref[...], x, (((1,), (1,)), ((), ())),
                                  preferred_element_type=F32).astype(BF16)


def _kv_proj(n, w_k, w_vt, k_gain, bd, head_dim, seq):
    m, d = n.shape
    kvd = w_k.shape[1]
    nkv = kvd // head_dim
    bm = _tile(seq, 512)
    return pl.pallas_call(
        functools.partial(_kv_kernel, head_dim=head_dim),
        grid=(m // bm,),
        in_specs=[
            pl.BlockSpec((bm, d), lambda i: (i, 0)),
            pl.BlockSpec((d, kvd), lambda i: (0, 0)),
            pl.BlockSpec((kvd, d), lambda i: (0, 0)),
            pl.BlockSpec((1, kvd), lambda i: (0, 0)),
            pl.BlockSpec((kvd, kvd), lambda i: (0, 0)),
        ],
        out_specs=[pl.BlockSpec((nkv, bm, head_dim), lambda i: (0, i, 0)),
                   pl.BlockSpec((kvd, bm), lambda i: (0, i))],
        out_shape=[jax.ShapeDtypeStruct((nkv, m, head_dim), BF16),
                   jax.ShapeDtypeStruct((kvd, m), BF16)],
        compiler_params=_params(("arbitrary",), 48),
        name="kv_proj",
    )(n, w_k, w_vt, k_gain, bd)


def _q_attn_kernel(n_ref, wqt_ref, gain_ref, kc_ref, kp_ref, vc_ref, vp_ref, bias_ref, sink_ref,
                   o_ref, *, seq, head_dim):
    i = pl.program_id(0)
    bm = n_ref.shape[0]
    group = wqt_ref.shape[0] // head_dim
    qt = lax.dot_general(wqt_ref[...], n_ref[...], (((1,), (1,)), ((), ())), preferred_element_type=F32)
    gain = gain_ref[...]
    qn = []
    for hh in range(group):
        qh = qt[hh * head_dim:(hh + 1) * head_dim, :]
        rs = lax.rsqrt(jnp.sum(qh * qh, axis=0, keepdims=True) * (1.0 / head_dim) + QK_EPS)
        qn.append((qh * rs * gain[hh * head_dim:(hh + 1) * head_dim, :]).astype(BF16))
    sink = sink_ref[...]
    shape = (WINDOW, group * WINDOW)
    key_c = lax.broadcasted_iota(I32, shape, 0)
    query_i = lax.broadcasted_iota(I32, shape, 1) & (WINDOW - 1)
    in_cur = key_c <= query_i
    for qb in range(bm // WINDOW):
        r0 = qb * WINDOW
        q_all = jnp.concatenate([q[:, r0:r0 + WINDOW] for q in qn], axis=1)
        if qb == 0:
            k_prev, vt_prev = kp_ref[0], vp_ref[...]
        else:
            k_prev, vt_prev = kc_ref[0, r0 - WINDOW:r0, :], vc_ref[:, r0 - WINDOW:r0]
        k2 = jnp.concatenate([k_prev, kc_ref[0, r0:r0 + WINDOW, :]], axis=0)
        v2t = jnp.concatenate([vt_prev, vc_ref[:, r0:r0 + WINDOW]], axis=1)
        st = jnp.dot(k2, q_all, preferred_element_type=F32)
        first = ((i * bm + r0) % seq == 0).astype(I32)
        s = jnp.where(in_cur, st[WINDOW:], st[:WINDOW]) + bias_ref[first]
        mx = jnp.maximum(jnp.max(s, axis=0, keepdims=True), sink)
        p = jnp.exp(s - mx)
        denom = jnp.sum(p, axis=0, keepdims=True) + jnp.exp(sink - mx)
        pt = jnp.concatenate([jnp.where(in_cur, 0.0, p), jnp.where(in_cur, p, 0.0)], axis=0).astype(BF16)
        ot = jnp.dot(v2t, pt, preferred_element_type=F32) * (1.0 / denom)
        for pr in range(group // 2):
            blk = jnp.concatenate([ot[:, (2 * pr) * WINDOW:(2 * pr + 1) * WINDOW],
                                   ot[:, (2 * pr + 1) * WINDOW:(2 * pr + 2) * WINDOW]], axis=0)
            o_ref[r0:r0 + WINDOW, pr * 2 * head_dim:(pr + 1) * 2 * head_dim] = blk.T.astype(BF16)


def _q_attn(n, w_qt, q_gain, k, vt, bias, sink, layer, head_dim, seq):
    m, d = n.shape
    nkv = k.shape[0]
    gw = w_qt.shape[1] // nkv
    group = gw // head_dim
    assert head_dim == WINDOW // 2 and group % 2 == 0
    bm = q_gain.shape[1]
    nb = bm // WINDOW
    return pl.pallas_call(
        functools.partial(_q_attn_kernel, seq=seq, head_dim=head_dim),
        grid=(m // bm, nkv),
        in_specs=[
            pl.BlockSpec((bm, d), lambda i, g: (i, 0)),
            pl.BlockSpec((None, gw, d), lambda i, g: (layer, g, 0)),
            pl.BlockSpec((gw, bm), lambda i, g: (0, 0)),
            pl.BlockSpec((1, bm, head_dim), lambda i, g: (g, i, 0)),
            pl.BlockSpec((1, WINDOW, head_dim), lambda i, g: (g, jnp.maximum(i * nb - 1, 0), 0)),
            pl.BlockSpec((head_dim, bm), lambda i, g: (g, i)),
            pl.BlockSpec((head_dim, WINDOW), lambda i, g: (g, jnp.maximum(i * nb - 1, 0))),
            pl.BlockSpec((None, 2, WINDOW, group * WINDOW), lambda i, g: (g, 0, 0, 0)),
            pl.BlockSpec((None, 1, group * WINDOW), lambda i, g: (g, 0, 0)),
        ],
        out_specs=pl.BlockSpec((bm, gw), lambda i, g: (i, g)),
        out_shape=jax.ShapeDtypeStruct((m, w_qt.shape[1]), BF16),
        compiler_params=_params(("arbitrary", "arbitrary"), 48),
        name="q_attn",
    )(n, w_qt, q_gain, k, k, vt, vt, bias, sink)


def _attn_bias(n_heads, nkv):
    group = n_heads // nkv
    h = np.arange(1, n_heads + 1, dtype=np.float32)
    slopes = np.power(2.0, -8.0 * h / n_heads).astype(np.float32)
    c = np.arange(WINDOW)[:, None]
    qi = np.arange(WINDOW)[None, :]
    in_cur = c <= qi
    dist = np.where(in_cur, qi - c, WINDOW + qi - c).astype(np.float32)
    alibi = -slopes[:, None, None] * dist[None]
    first = np.where(in_cur[None], alibi, -np.inf).astype(np.float32)
    tables = np.stack([alibi, first], axis=1)
    tables = tables.reshape(nkv, group, 2, WINDOW, WINDOW).transpose(0, 2, 3, 1, 4)
    return jnp.asarray(tables.reshape(nkv, 2, WINDOW, group * WINDOW))


def _expert_ffn(x_ref, y_ref, wgu_ref, bgu_ref, wd_ref, bd_ref):
    lo, hi = _unpack_bf16_pair(x_ref[...])
    half = lo.shape[1]
    f = wd_ref.shape[1]
    x = jnp.concatenate([lo.astype(BF16), hi.astype(BF16)], axis=1)
    kc = x.shape[1] // MOE_IN_CHUNKS
    f_split = [(0, min(f, MXU_DEPTH))] + ([(MXU_DEPTH, f)] if f > MXU_DEPTH else [])
    hids = []
    for f0, f1 in f_split:
        bias = jnp.concatenate([bgu_ref[0, :, f0:f1], bgu_ref[0, :, f + f0:f + f1]], axis=1)
        row_parts = 2 if bias.shape[1] <= MXU_DEPTH else 1
        rp = x.shape[0] // row_parts
        parts = [bias] * row_parts
        for ch in range(MOE_IN_CHUNKS):
            k0 = ch * kc
            w = jnp.concatenate([wgu_ref[0, k0:k0 + kc, f0:f1], wgu_ref[0, k0:k0 + kc, f + f0:f + f1]],
                                axis=1).astype(BF16)
            parts = [p + jnp.dot(x[i * rp:(i + 1) * rp, k0:k0 + kc], w, preferred_element_type=F32)
                     for i, p in enumerate(parts)]
        gl = parts[0] if row_parts == 1 else jnp.concatenate(parts, axis=0)
        glu = jnp.minimum(gl[:, :f1 - f0], SWIGLU_LIMIT)
        lin = jnp.clip(gl[:, f1 - f0:], -SWIGLU_LIMIT, SWIGLU_LIMIT)
        hids.append((glu * jax.nn.sigmoid(SWIGLU_ALPHA * glu) * (lin + 1.0)).astype(BF16))
    cw = half // MOE_OUT_CHUNKS
    for ch in range(MOE_OUT_CHUNKS):
        a0, b0 = ch * cw, half + ch * cw
        ya = bd_ref[0, :, a0:a0 + cw]
        yb = bd_ref[0, :, b0:b0 + cw]
        for (f0, f1), hid in zip(f_split, hids):
            ya = ya + jnp.dot(hid, wd_ref[0, f0:f1, a0:a0 + cw].astype(BF16), preferred_element_type=F32)
            yb = yb + jnp.dot(hid, wd_ref[0, f0:f1, b0:b0 + cw].astype(BF16), preferred_element_type=F32)
        y_ref[:, a0:a0 + cw] = _pack_bf16_pair(ya, yb)


def _dispatch_kernel(pos_ref, n_ref, xs_in, xs_out, sem, *, bt):
    for r in range(bt):
        for k in range(TOP_K):
            p = pos_ref[0, 0, r * TOP_K + k]
            pltpu.make_async_copy(n_ref.at[pl.ds(r, 1)], xs_out.at[pl.ds(p, 1)], sem).start(priority=k % 2)
    pltpu.make_async_copy(xs_in.at[pl.ds(0, bt * TOP_K)], xs_out.at[pl.ds(0, bt * TOP_K)], sem).wait()


def _dispatch(pos, n_packed, xs):
    m, dh = n_packed.shape
    bt = _tile(m, 512)
    return pl.pallas_call(
        functools.partial(_dispatch_kernel, bt=bt),
        grid=(m // bt,),
        in_specs=[
            pl.BlockSpec((1, 1, bt * TOP_K), lambda i: (i, 0, 0), memory_space=pltpu.SMEM),
            pl.BlockSpec((bt, dh), lambda i: (i, 0)),
            pl.BlockSpec(memory_space=pl.ANY),
        ],
        out_specs=pl.BlockSpec(memory_space=pl.ANY),
        out_shape=jax.ShapeDtypeStruct(xs.shape, xs.dtype),
        scratch_shapes=[pltpu.SemaphoreType.DMA(())],
        input_output_aliases={2: 0},
        compiler_params=_params(("arbitrary",), 24),
        name="moe_dispatch",
    )(pos.reshape(m // bt, 1, bt * TOP_K), n_packed, xs)


def _moe_kernel(te_ref, nv_ref, xs_ref, wgu_ref, bgu_ref, wd_ref, bd_ref, ys_ref):
    del te_ref
    valid = pl.program_id(0) < nv_ref[0]

    @pl.when(jnp.logical_not(valid))
    def _():
        ys_ref[...] = jnp.zeros_like(ys_ref)

    @pl.when(valid)
    def _():
        _expert_ffn(xs_ref, ys_ref, wgu_ref, bgu_ref, wd_ref, bd_ref)


def _moe_grouped(tile_expert, n_valid, xs, w_gu, b_gu, w_d, b_d, layer, tm):
    rows, dh = xs.shape
    _, ne, d, f2 = w_gu.shape
    f = f2 // 2
    assert dh % MOE_OUT_CHUNKS == 0
    by_expert = lambda j, te, nv: (layer, te[j], 0, 0)
    grid_spec = pltpu.PrefetchScalarGridSpec(
        num_scalar_prefetch=2,
        grid=(rows // tm,),
        in_specs=[
            pl.BlockSpec((tm, dh), lambda j, te, nv: (j, 0)),
            pl.BlockSpec((None, 1, d, f2), by_expert),
            pl.BlockSpec((None, 1, 1, f2), by_expert),
            pl.BlockSpec((None, 1, f, d), by_expert),
            pl.BlockSpec((None, 1, 1, d), by_expert),
        ],
        out_specs=pl.BlockSpec((tm, dh), lambda j, te, nv: (j, 0)),
    )
    return pl.pallas_call(
        _moe_kernel,
        grid_spec=grid_spec,
        out_shape=jax.ShapeDtypeStruct((rows, dh), U32),
        compiler_params=_params(("arbitrary",), 56),
        name="moe_grouped",
    )(tile_expert, n_valid, xs, w_gu, b_gu, w_d, b_d)


COMBINE_ROWS = 16
COMBINE_WORDS = 256


def _combine_kernel(*refs, n_norms):
    pos_ref, pos_next_ref, h_ref, wt_ref, g_ref = refs[:5]
    norm_refs = refs[5:5 + 2 * n_norms]
    ys_hbm = refs[5 + 2 * n_norms]
    o_ref = refs[6 + 2 * n_norms]
    n_refs = refs[7 + 2 * n_norms:7 + 3 * n_norms]
    buf0, buf1, sem = refs[7 + 3 * n_norms:]
    bc, d = h_ref.shape
    dh = d // 2
    rows, cw = COMBINE_ROWS, COMBINE_WORDS
    i = pl.program_id(0)
    last = pl.num_programs(0) - 1

    def gather_rows(p_ref, buf, s):
        for r in range(bc):
            for k in range(TOP_K):
                pltpu.make_async_copy(ys_hbm.at[pl.ds(p_ref[0, 0, r * TOP_K + k], 1)],
                                      buf.at[k, pl.ds(r, 1)], s).start()

    @pl.when(i == 0)
    def _():
        gather_rows(pos_ref, buf0, sem.at[0])

    def step(par):
        cur, nxt = (buf0, buf1) if par == 0 else (buf1, buf0)

        @pl.when(i < last)
        def _():
            gather_rows(pos_next_ref, nxt, sem.at[1 - par])

        for k in range(TOP_K):
            pltpu.make_async_copy(ys_hbm.at[pl.ds(0, bc)], cur.at[k], sem.at[par]).wait()
        lax.fori_loop(0, bc // rows, functools.partial(combine_rows, cur), 0)

    def combine_rows(y_ref, r, carry):
        rs = pl.ds(pl.multiple_of(r * rows, rows), rows)
        wt = wt_ref[rs, :]
        wk = [jnp.broadcast_to(wt[:, k:k + 1], (rows, cw)) for k in range(TOP_K)]
        ssq = jnp.zeros((rows, LANES), F32)
        for c in range(dh // cw):
            acc_lo = acc_hi = None
            for k in range(TOP_K):
                lo, hi = _unpack_bf16_pair(y_ref[k, rs, c * cw:(c + 1) * cw])
                acc_lo = wk[k] * lo if acc_lo is None else acc_lo + wk[k] * lo
                acc_hi = wk[k] * hi if acc_hi is None else acc_hi + wk[k] * hi
            a0, b0 = c * cw, dh + c * cw
            ha = h_ref[rs, a0:a0 + cw] + g_ref[0, :, a0:a0 + cw] * acc_lo
            hb = h_ref[rs, b0:b0 + cw] + g_ref[0, :, b0:b0 + cw] * acc_hi
            o_ref[rs, a0:a0 + cw] = ha
            o_ref[rs, b0:b0 + cw] = hb
            if n_norms:
                for t in range(cw // LANES):
                    va = ha[:, t * LANES:(t + 1) * LANES]
                    vb = hb[:, t * LANES:(t + 1) * LANES]
                    ssq = ssq + va * va + vb * vb
        if n_norms:
            rsq = lax.rsqrt(jnp.sum(ssq, axis=-1, keepdims=True) * (1.0 / d) + NORM_EPS)
            for c in range(d // (2 * cw)):
                cs = slice(c * 2 * cw, (c + 1) * 2 * cw)
                xn = o_ref[rs, cs] * rsq
                for t in range(n_norms):
                    gs_ref, sh_ref = norm_refs[2 * t], norm_refs[2 * t + 1]
                    n_refs[t][rs, cs] = (xn * gs_ref[0, :, cs] + sh_ref[0, :, cs]).astype(BF16)
        return carry

    @pl.when(i % 2 == 0)
    def _():
        step(0)

    @pl.when(i % 2 == 1)
    def _():
        step(1)


def _combine(pos, ys, h, wt, gate, norms, seq):
    m, d = h.shape
    bsz = gate.shape[0]
    bc = _tile(seq, 256)
    nblk = m // bc
    pos_blocks = pos.reshape(nblk, 1, bc * TOP_K)
    pos_tile = lambda shift: pl.BlockSpec(
        (1, 1, bc * TOP_K), lambda i: (jnp.minimum(i + shift, nblk - 1), 0, 0), memory_space=pltpu.SMEM)
    per_batch = pl.BlockSpec((1, 1, d), lambda i: ((i * bc) // seq, 0, 0))
    row_tile = pl.BlockSpec((bc, d), lambda i: (i, 0))
    norm_args = [v.reshape(bsz, 1, d) for pair in norms for v in pair]
    outs = pl.pallas_call(
        functools.partial(_combine_kernel, n_norms=len(norms)),
        grid=(nblk,),
        in_specs=[
            pos_tile(0),
            pos_tile(1),
            row_tile,
            pl.BlockSpec((bc, LANES), lambda i: (i, 0)),
            per_batch,
        ] + [per_batch] * len(norm_args) + [pl.BlockSpec(memory_space=pl.ANY)],
        out_specs=[row_tile] * (1 + len(norms)),
        out_shape=[jax.ShapeDtypeStruct((m, d), F32)] + [jax.ShapeDtypeStruct((m, d), BF16)] * len(norms),
        scratch_shapes=[pltpu.VMEM((TOP_K, bc, d // 2), U32)] * 2 + [pltpu.SemaphoreType.DMA((2,))],
        compiler_params=_params(("arbitrary",), 56),
        name="moe_combine",
    )(pos_blocks, pos_blocks, h, wt, gate.reshape(bsz, 1, d), *norm_args, ys)
    return outs[0], list(outs[1:])


def _moe_layer(h, gain, scale, shift, gate, rw, rb, w_gu, b_gu, w_d, b_d, xs, next_norms, layer, seq, tm):
    m, d = h.shape
    ne = w_gu.shape[1]
    n_packed, route, wt, counts = _norm_router(h, gain, scale, shift, rw, rb, layer, seq)

    cnt = counts[0, :ne].astype(I32)
    padded = ((cnt + tm - 1) // tm) * tm
    ends = jnp.cumsum(padded)
    starts = ends - padded
    expert_ids = jnp.arange(ne, dtype=I32)
    idx = route[:, :TOP_K]
    pos = jnp.sum(jnp.where(idx[:, :, None] == expert_ids, starts, 0), axis=-1) + route[:, TOP_K:2 * TOP_K]
    n_tiles = xs.shape[0] // tm
    n_valid = ends[-1] // tm
    tile_id = jnp.arange(n_tiles, dtype=I32)
    tile_start = jnp.minimum(tile_id, n_valid - 1) * tm
    tile_expert = jnp.sum((ends[None, :] <= tile_start[:, None]).astype(I32), axis=1)

    xs = _dispatch(pos, n_packed, xs)
    ys = _moe_grouped(tile_expert, n_valid.reshape(1).astype(I32), xs, w_gu, b_gu, w_d, b_d, layer, tm)
    h, normed = _combine(pos, ys, h, wt, gate, next_norms, seq)
    return h, normed, xs


def kernel(x, c, norm_mix, norm_ffn, mod_down, mod_up, mod_b, a_w_in, a_conv, a_w_out, kv_norm, kv_mod_down, kv_mod_up, kv_mod_b, w_kv, k_norm, b_w_q, b_q_norm, b_sinks, b_w_o, router_w, router_b, moe_w_gu, moe_b_gu, moe_w_down, moe_b_down):
    bsz, seq, d = x.shape
    m = bsz * seq
    depth = norm_mix.shape[0]
    n_a = a_w_in.shape[0]
    head_dim = k_norm.shape[0]
    n_heads = b_sinks.shape[1]
    nkv = w_kv.shape[1] // (2 * head_dim)
    group = n_heads // nkv
    ne = router_w.shape[2]
    tm = _tile(m * TOP_K, 256)
    attn_bm = _tile(seq, 512)
    assert seq % WINDOW == 0 and n_a >= 1

    mods = _adaln(c, mod_down, mod_up, mod_b)
    kv_mods = _adaln(c, kv_mod_down[None], kv_mod_up[None], kv_mod_b[None])[0]

    w_in = a_w_in.astype(BF16)
    w_out = a_w_out.astype(BF16)
    w_qt = jnp.swapaxes(b_w_q, 1, 2).astype(BF16)
    w_o = b_w_o.astype(BF16)
    kvd = nkv * head_dim
    w_k = w_kv[:, :kvd].astype(BF16)
    w_vt = w_kv[:, kvd:].T.astype(BF16)
    w_gu = moe_w_gu
    w_d = moe_w_down
    b_gu = moe_b_gu[:, :, None, :]
    b_d = moe_b_down[:, :, None, :]
    rw = jnp.zeros((depth, d, LANES), BF16).at[:, :, :ne].set(router_w.astype(BF16))
    rb = jnp.full((depth, 1, LANES), NEG_BIG, F32).at[:, 0, :ne].set(router_b)

    gw = group * head_dim
    bd_kv = jnp.asarray(np.kron(np.eye(nkv, dtype=np.float32),
                                np.ones((head_dim, head_dim), np.float32)), BF16)
    attn_bias = _attn_bias(n_heads, nkv)

    h = x.reshape(m, d)
    xs = jnp.zeros((m * TOP_K + ne * tm, d // 2), U32)
    sh_kv, sc_kv = jnp.split(kv_mods, 2, axis=-1)
    k = vt = None
    sh_m, sc_m = mods[0, :, :d], mods[0, :, d:2 * d]
    n = _norm_mod(h, norm_mix[0], sc_m, sh_m, seq)
    n_kv = None
    for layer in range(depth):
        _, _, g_m, sh_f, sc_f, g_f = jnp.split(mods[layer], 6, axis=-1)
        if layer < n_a:
            y = _conv_in(n, w_in, a_conv, layer, seq)
            h = _matmul_res(y, w_out, h, g_m, layer, seq)
        else:
            if layer == n_a:
                k, vt = _kv_proj(n_kv, w_k, w_vt, jnp.tile(k_norm, nkv).reshape(1, kvd), bd_kv, head_dim, seq)
            i = layer - n_a
            q_gain = jnp.broadcast_to((jnp.tile(b_q_norm[i], group) * head_dim ** -0.5)[:, None], (gw, attn_bm))
            sink = jnp.repeat(b_sinks[i].reshape(nkv, group), WINDOW, axis=1).reshape(nkv, 1, group * WINDOW)
            a = _q_attn(n, w_qt, q_gain, k, vt, attn_bias, sink, i, head_dim, seq)
            h = _matmul_res(a, w_o, h, g_m, i, seq)
        next_norms = []
        if layer + 1 < depth:
            sh_n, sc_n = mods[layer + 1, :, :d], mods[layer + 1, :, d:2 * d]
            next_norms.append((norm_mix[layer + 1] * (1.0 + sc_n), sh_n))
            if layer + 1 == n_a:
                next_norms.append((kv_norm * (1.0 + sc_kv), sh_kv))
        h, normed, xs = _moe_layer(h, norm_ffn[layer], sc_f, sh_f, g_f, rw, rb, w_gu, b_gu, w_d, b_d, xs,
                                   next_norms, layer, seq, tm)
        if normed:
            n = normed[0]
            n_kv = normed[1] if len(normed) > 1 else None
    return h.reshape(bsz, seq, d)
```
